```python
import jax, jax.numpy as jnp
from jax import lax
import numpy as np

D_MODEL = 4096
BATCH = 4
SEQ = 2048
DEPTH = 1
DEC_BATCH = 128
DEC_SEQ = 1
PAST_LEN = 2048
PAGE_SIZE = 128

HEAD_DIM = 128
N_ATT_HEADS = 16
ATT_WIDTH = N_ATT_HEADS * HEAD_DIM
CONV_WIDTH = D_MODEL - ATT_WIDTH
N_CONV_GROUPS = CONV_WIDTH // HEAD_DIM
MIX_WIDTH = ATT_WIDTH + CONV_WIDTH
CONV_K = 3
D_FF = 4 * D_MODEL
PLE_DIM = 256
Q_BLOCK = 128
EPS = 1e-6
NEG_INF = -1e30
IN_WIDTH = 3 * ATT_WIDTH + N_ATT_HEADS + 3 * CONV_WIDTH

kernel_name = "hybrid_fox_shortconv_decode_step"


def rmsnorm(x, g):
    xf = x.astype(jnp.float32)
    y = xf * lax.rsqrt(jnp.mean(xf * xf, axis=-1, keepdims=True) + EPS)
    return (y * g.astype(jnp.float32)).astype(x.dtype)


def project(u, w_in, b_f, q_gain, k_gain):
    z = u @ w_in
    a, h, c = ATT_WIDTH, N_ATT_HEADS, CONV_WIDTH
    q, k, v, f, gb, gc, hc = jnp.split(
        z, [a, 2 * a, 3 * a, 3 * a + h, 3 * a + h + c, 3 * a + h + 2 * c], axis=-1)
    shp = z.shape[:-1] + (N_ATT_HEADS, HEAD_DIM)
    q = rmsnorm(q.reshape(shp), q_gain)
    k = rmsnorm(k.reshape(shp), k_gain)
    v = v.reshape(shp)
    logf = jax.nn.log_sigmoid((f + b_f).astype(jnp.float32))
    conv_in = gc * hc
    return q, k, v, logf, gb, conv_in


def forget_attention(q, k, v, cq, ck, q_pos, k_pos):
    s = jnp.einsum('bqhd,bkhd->bhqk', q, k, preferred_element_type=jnp.float32) * (HEAD_DIM ** -0.5)
    decay = jnp.transpose(cq, (0, 2, 1))[:, :, :, None] - jnp.transpose(ck, (0, 2, 1))[:, :, None, :]
    s = s + decay
    mask = k_pos[None, :] <= q_pos[:, None]
    s = jnp.where(mask[None, None], s, NEG_INF)
    p = jax.nn.softmax(s, axis=-1)
    return jnp.einsum('bhqk,bkhd->bqhd', p.astype(v.dtype), v)


def prompt_attention(q, k, v, logf):
    b, t = q.shape[:2]
    c = jnp.cumsum(logf, axis=1)
    pos = jnp.arange(t)

    def block(i):
        s0 = i * Q_BLOCK
        qb = lax.dynamic_slice_in_dim(q, s0, Q_BLOCK, 1)
        cb = lax.dynamic_slice_in_dim(c, s0, Q_BLOCK, 1)
        pb = lax.dynamic_slice_in_dim(pos, s0, Q_BLOCK, 0)
        return forget_attention(qb, k, v, cb, c, pb, pos)

    o = lax.map(block, jnp.arange(t // Q_BLOCK))
    return jnp.transpose(o, (1, 0, 2, 3, 4)).reshape(b, t, ATT_WIDTH)


def sample_attention(q, k, v, logf, ck_pool, cv_pool, clf_pool, page_table):
    db, t = q.shape[:2]
    past = page_table.shape[1] * PAGE_SIZE
    kp = ck_pool[page_table].reshape(db, past, N_ATT_HEADS, HEAD_DIM)
    vp = cv_pool[page_table].reshape(db, past, N_ATT_HEADS, HEAD_DIM)
    lp = clf_pool[page_table].reshape(db, past, N_ATT_HEADS).astype(jnp.float32)
    k_all = jnp.concatenate([kp, k.astype(kp.dtype)], axis=1)
    v_all = jnp.concatenate([vp, v.astype(vp.dtype)], axis=1)
    c = jnp.cumsum(jnp.concatenate([lp, logf], axis=1), axis=1)
    k_pos = jnp.arange(past + t)
    q_pos = past + jnp.arange(t)
    o = forget_attention(q, k_all, v_all, c[:, past:], c, q_pos, k_pos)
    return o.reshape(db, t, ATT_WIDTH)


def short_conv(z_hist, w):
    t = z_hist.shape[1] - (CONV_K - 1)
    y = w[0] * z_hist[:, 0:t]
    for j in range(1, CONV_K):
        y = y + w[j] * z_hist[:, j:j + t]
    return y


def mixer_out(attn_o, conv_hist, gb, conv_w, g_ao, g_co, w_out):
    conv_o = gb * short_conv(conv_hist, conv_w)
    o = jnp.concatenate([rmsnorm(attn_o, g_ao), rmsnorm(conv_o, g_co)], axis=-1)
    return o @ w_out


def channel_and_ple(h, p, g_mlp, w_up, w_down, g_ple, w_pg, w_pe):
    u = rmsnorm(h, g_mlp)
    h = h + jnp.square(jax.nn.relu(u @ w_up)) @ w_down
    gate = jax.nn.sigmoid(rmsnorm(h, g_ple) @ w_pg)
    return h + gate * (p @ w_pe)


def setup_inputs(seed: int = 0) -> dict:
    key = jax.random.key(seed)
    ks = jax.random.split(key, 28)
    n_pages = PAST_LEN // PAGE_SIZE
    n_pool = (DEC_BATCH * n_pages * 5) // 4

    def nrm(k, shape, scale=1.0):
        return scale * jax.random.normal(k, shape, jnp.float32)

    def gain(k, shape):
        return 1.0 + nrm(k, shape, 0.02)

    page_table = jax.random.permutation(ks[6], n_pool)[:DEC_BATCH * n_pages]
    page_table = page_table.reshape(DEC_BATCH, n_pages).astype(jnp.int32)
    return {
        "x_prompt": nrm(ks[0], (BATCH, SEQ, D_MODEL)),
        "x_sample": nrm(ks[1], (DEC_BATCH, DEC_SEQ, D_MODEL)),
        "cache_k": nrm(ks[2], (DEPTH, n_pool, PAGE_SIZE, N_ATT_HEADS, HEAD_DIM)),
        "cache_v": nrm(ks[3], (DEPTH, n_pool, PAGE_SIZE, N_ATT_HEADS, HEAD_DIM)),
        "cache_logf": jax.nn.log_sigmoid(1.0 + nrm(ks[4], (DEPTH, n_pool, PAGE_SIZE, N_ATT_HEADS))),
        "state_conv": nrm(ks[5], (DEPTH, DEC_BATCH, CONV_K - 1, CONV_WIDTH)),
        "page_table": page_table,
        "p_prompt": nrm(ks[7], (DEPTH, BATCH, SEQ, PLE_DIM)),
        "p_sample": nrm(ks[8], (DEPTH, DEC_BATCH, DEC_SEQ, PLE_DIM)),
        "g_mix": gain(ks[9], (DEPTH, D_MODEL)),
        "w_in": nrm(ks[10], (DEPTH, D_MODEL, IN_WIDTH), D_MODEL ** -0.5),
        "b_f": 1.0 + nrm(ks[11], (DEPTH, N_ATT_HEADS), 0.1),
        "q_gain": gain(ks[12], (DEPTH, HEAD_DIM)),
        "k_gain": gain(ks[13], (DEPTH, HEAD_DIM)),
        "conv_w": nrm(ks[14], (DEPTH, CONV_K, CONV_WIDTH), CONV_K ** -0.5),
        "g_attn_out": gain(ks[15], (DEPTH, ATT_WIDTH)),
        "g_conv_out": gain(ks[16], (DEPTH, CONV_WIDTH)),
        "w_out": nrm(ks[17], (DEPTH, MIX_WIDTH, D_MODEL), MIX_WIDTH ** -0.5),
        "g_mlp": gain(ks[18], (DEPTH, D_MODEL)),
        "w_up": nrm(ks[19], (DEPTH, D_MODEL, D_FF), D_MODEL ** -0.5),
        "w_down": nrm(ks[20], (DEPTH, D_FF, D_MODEL), D_FF ** -0.5),
        "g_ple": gain(ks[21], (DEPTH, D_MODEL)),
        "w_pg": nrm(ks[22], (DEPTH, D_MODEL, D_MODEL), D_MODEL ** -0.5),
        "w_pe": nrm(ks[23], (DEPTH, PLE_DIM, D_MODEL), PLE_DIM ** -0.5),
    }


def reference(x_prompt, x_sample, cache_k, cache_v, cache_logf, state_conv, page_table,
              p_prompt, p_sample, g_mix, w_in, b_f, q_gain, k_gain, conv_w,
              g_attn_out, g_conv_out, w_out, g_mlp, w_up, w_down, g_ple, w_pg, w_pe):
    hp, hs = x_prompt, x_sample
    kp_l, vp_l, lp_l, cp_l = [], [], [], []
    ks_l, vs_l, ls_l, cs_l = [], [], [], []
    for i in range(DEPTH):
        u = rmsnorm(hp, g_mix[i])
        q, k, v, lf, gb, ci = project(u, w_in[i], b_f[i], q_gain[i], k_gain[i])
        ao = prompt_attention(q, k, v, lf)
        hist = jnp.pad(ci, ((0, 0), (CONV_K - 1, 0), (0, 0)))
        hp = hp + mixer_out(ao, hist, gb, conv_w[i], g_attn_out[i], g_conv_out[i], w_out[i])
        hp = channel_and_ple(hp, p_prompt[i], g_mlp[i], w_up[i], w_down[i], g_ple[i], w_pg[i], w_pe[i])
        kp_l.append(k); vp_l.append(v); lp_l.append(lf); cp_l.append(hist[:, -(CONV_K - 1):])

        u = rmsnorm(hs, g_mix[i])
        q, k, v, lf, gb, ci = project(u, w_in[i], b_f[i], q_gain[i], k_gain[i])
        ao = sample_attention(q, k, v, lf, cache_k[i], cache_v[i], cache_logf[i], page_table)
        hist = jnp.concatenate([state_conv[i].astype(ci.dtype), ci], axis=1)
        hs = hs + mixer_out(ao, hist, gb, conv_w[i], g_attn_out[i], g_conv_out[i], w_out[i])
        hs = channel_and_ple(hs, p_sample[i], g_mlp[i], w_up[i], w_down[i], g_ple[i], w_pg[i], w_pe[i])
        ks_l.append(k); vs_l.append(v); ls_l.append(lf); cs_l.append(hist[:, -(CONV_K - 1):])

    return (hp, hs,
            jnp.stack(kp_l), jnp.stack(vp_l), jnp.stack(lp_l), jnp.stack(cp_l),
            jnp.stack(ks_l), jnp.stack(vs_l), jnp.stack(ls_l), jnp.stack(cs_l))
```

```python
import functools

import jax
import jax.numpy as jnp
from jax import lax
from jax.experimental import pallas as pl
from jax.experimental.pallas import tpu as pltpu

F32 = jnp.float32
BF16 = jnp.bfloat16

D_MODEL = 4096
HEAD_DIM = 128
N_HEADS = 16
ATT_WIDTH = N_HEADS * HEAD_DIM
CONV_WIDTH = D_MODEL - ATT_WIDTH
CONV_K = 3
D_FF = 4 * D_MODEL
PLE_DIM = 256
PAGE_SIZE = 128
EPS = 1e-6
NEG_INF = -1e30
SCALE = HEAD_DIM ** -0.5
LANES = 128
HEAD_TILE = 6 * HEAD_DIM
VMEM_LIMIT = 56 * 1024 * 1024
HIGHEST = lax.Precision.HIGHEST


def _params(*sem):
    return pltpu.CompilerParams(dimension_semantics=sem, vmem_limit_bytes=VMEM_LIMIT)


def _log_sigmoid(x):
    return jnp.minimum(x, 0.0) - jnp.log1p(jnp.exp(-jnp.abs(x)))


def _row_rms_scale(x):
    return lax.rsqrt(jnp.mean(x * x, axis=-1, keepdims=True) + EPS)


def _rmsnorm_kernel(x_ref, g_ref, o_ref):
    x = x_ref[...]
    o_ref[...] = ((x * _row_rms_scale(x)) * g_ref[...]).astype(o_ref.dtype)


def rmsnorm_bf16(x, g, tm):
    m, d = x.shape
    return pl.pallas_call(
        _rmsnorm_kernel,
        grid=(m // tm,),
        in_specs=[pl.BlockSpec((tm, d), lambda i: (i, 0)),
                  pl.BlockSpec((1, d), lambda i: (0, 0))],
        out_specs=pl.BlockSpec((tm, d), lambda i: (i, 0)),
        out_shape=jax.ShapeDtypeStruct((m, d), BF16),
        compiler_params=_params("parallel"),
        name="rmsnorm_bf16",
    )(x, g.reshape(1, d))


def _inproj_kernel(u_ref, w_ref, wf_ref, bf_ref, qg_ref, kg_ref,
                   q_ref, kf_ref, kb_ref, vf_ref, vb_ref, gb_ref, ci_ref, lf_ref, *, tm, rc):
    d = HEAD_DIM
    for r in range(tm // rc):
        rows = slice(r * rc, (r + 1) * rc)
        z = jnp.dot(u_ref[rows, :], w_ref[...], preferred_element_type=F32)
        zq, zk, zv = z[:, 0:d], z[:, d:2 * d], z[:, 2 * d:3 * d]
        q = (zq * _row_rms_scale(zq)) * qg_ref[...]
        k = (zk * _row_rms_scale(zk)) * kg_ref[...]
        q_ref[rows, :] = q.astype(BF16)
        kf_ref[rows, :] = k
        kb_ref[rows, :] = k.astype(BF16)
        vf_ref[rows, :] = zv
        vb_ref[rows, :] = zv.astype(BF16)
        gb_ref[rows, :] = z[:, 3 * d:4 * d]
        ci_ref[rows, :] = z[:, 4 * d:5 * d] * z[:, 5 * d:6 * d]

    @pl.when(pl.program_id(1) == 0)
    def _():
        for r in range(tm // rc):
            rows = slice(r * rc, (r + 1) * rc)
            f = jnp.dot(u_ref[rows, :], wf_ref[...], preferred_element_type=F32) + bf_ref[...]
            lf_ref[rows, :] = _log_sigmoid(f)


def in_projection(u, w_heads, w_f, b_f, q_gain, k_gain, tm):
    m, dm = u.shape
    rc = min(tm, 256)
    row_head = lambda i, h: (i, h)
    fixed = lambda i, h: (0, 0)
    head_out = pl.BlockSpec((tm, HEAD_DIM), row_head)
    shapes = [jax.ShapeDtypeStruct((m, ATT_WIDTH), dt)
              for dt in (BF16, F32, BF16, F32, BF16, F32, F32)]
    shapes.append(jax.ShapeDtypeStruct((m, LANES), F32))
    return pl.pallas_call(
        functools.partial(_inproj_kernel, tm=tm, rc=rc),
        grid=(m // tm, N_HEADS),
        in_specs=[pl.BlockSpec((tm, dm), lambda i, h: (i, 0)),
                  pl.BlockSpec((dm, HEAD_TILE), lambda i, h: (0, h)),
                  pl.BlockSpec((dm, LANES), fixed),
                  pl.BlockSpec((1, LANES), fixed),
                  pl.BlockSpec((1, HEAD_DIM), fixed),
                  pl.BlockSpec((1, HEAD_DIM), fixed)],
        out_specs=[head_out] * 7 + [pl.BlockSpec((tm, LANES), lambda i, h: (i, 0))],
        out_shape=shapes,
        compiler_params=_params("parallel", "arbitrary"),
        name="in_projection",
    )(u, w_heads, w_f, b_f, q_gain.reshape(1, HEAD_DIM), k_gain.reshape(1, HEAD_DIM))


def _cumsum_kernel(lf_ref, c_ref, *, t):
    r = lax.broadcasted_iota(jnp.int32, (LANES, LANES), 0)
    c = lax.broadcasted_iota(jnp.int32, (LANES, LANES), 1)
    tri = (c <= r).astype(F32)
    carry = jnp.zeros((1, LANES), F32)
    for j in range(t // LANES):
        rows = slice(j * LANES, (j + 1) * LANES)
        cs = jnp.dot(tri, lf_ref[rows, :], precision=HIGHEST, preferred_element_type=F32) + carry
        c_ref[rows, :] = cs
        carry = cs[LANES - 1:LANES, :]


def cumsum_time(lf, batch, t):
    return pl.pallas_call(
        functools.partial(_cumsum_kernel, t=t),
        grid=(batch,),
        in_specs=[pl.BlockSpec((t, LANES), lambda b: (b, 0))],
        out_specs=pl.BlockSpec((t, LANES), lambda b: (b, 0)),
        out_shape=jax.ShapeDtypeStruct(lf.shape, F32),
        compiler_params=_params("parallel"),
        name="cumsum_time",
    )(lf)


def _flash_kernel(q_ref, k_ref, v_ref, c_ref, ct_ref, o_ref, m_scr, l_scr, acc_scr, *, tq, tk):
    qi = pl.program_id(1)
    kj = pl.program_id(2)
    d = HEAD_DIM

    @pl.when(kj == 0)
    def _():
        m_scr[...] = jnp.full(m_scr.shape, NEG_INF, F32)
        l_scr[...] = jnp.zeros(l_scr.shape, F32)
        acc_scr[...] = jnp.zeros(acc_scr.shape, F32)

    def step(masked):
        if masked:
            row = lax.broadcasted_iota(jnp.int32, (tq, tk), 0)
            col = lax.broadcasted_iota(jnp.int32, (tq, tk), 1)
            keep = col <= row
        for h in range(N_HEADS):
            cs = slice(h * d, (h + 1) * d)
            s = lax.dot_general(q_ref[:, cs], k_ref[:, cs], (((1,), (1,)), ((), ())),
                                preferred_element_type=F32)
            s = s * SCALE + (c_ref[:, h:h + 1] - ct_ref[0, h:h + 1, :])
            if masked:
                s = jnp.where(keep, s, NEG_INF)
            m_old = m_scr[h]
            m_new = jnp.maximum(m_old, jnp.max(s, axis=-1, keepdims=True))
            p = jnp.exp(s - pltpu.repeat(m_new, tk // LANES, axis=1))
            alpha = jnp.exp(m_old - m_new)
            l_scr[h] = alpha * l_scr[h] + jnp.sum(p, axis=-1, keepdims=True)
            m_scr[h] = m_new
            acc_scr[:, cs] = alpha * acc_scr[:, cs] + jnp.dot(
                p.astype(BF16), v_ref[:, cs], preferred_element_type=F32)

    @pl.when(kj < qi)
    def _():
        step(False)

    @pl.when(kj == qi)
    def _():
        step(True)
        for h in range(N_HEADS):
            cs = slice(h * d, (h + 1) * d)
            o_ref[:, cs] = acc_scr[:, cs] / l_scr[h]


def prompt_attention(q, k, v, c, ct, batch, t, tq):
    m = q.shape[0]
    nq = t // tq
    qrow = lambda b, i, j: (b * nq + i, 0)
    krow = lambda b, i, j: (b * nq + jnp.minimum(i, j), 0)
    return pl.pallas_call(
        functools.partial(_flash_kernel, tq=tq, tk=tq),
        grid=(batch, nq, nq),
        in_specs=[pl.BlockSpec((tq, ATT_WIDTH), qrow),
                  pl.BlockSpec((tq, ATT_WIDTH), krow),
                  pl.BlockSpec((tq, ATT_WIDTH), krow),
                  pl.BlockSpec((tq, LANES), qrow),
                  pl.BlockSpec((1, N_HEADS, tq), lambda b, i, j: (b, 0, jnp.minimum(i, j)))],
        out_specs=pl.BlockSpec((tq, ATT_WIDTH), qrow),
        out_shape=jax.ShapeDtypeStruct((m, ATT_WIDTH), F32),
        scratch_shapes=[pltpu.VMEM((N_HEADS, tq, LANES), F32),
                        pltpu.VMEM((N_HEADS, tq, LANES), F32),
                        pltpu.VMEM((tq, ATT_WIDTH), F32)],
        compiler_params=_params("parallel", "parallel", "arbitrary"),
        name="prompt_attention",
    )(q, k, v, c, ct)


PAGES_PER_STEP = 4


def _decode_kernel(pt_ref, q_ref, kn_ref, vn_ref, lfn_ref, *refs, n_steps):
    del pt_ref
    npg = PAGES_PER_STEP
    k_refs, v_refs, lp_refs = refs[0:npg], refs[npg:2 * npg], refs[2 * npg:3 * npg]
    o_ref, lpad_scr, m_scr, l_scr, tot_scr, acc_scr = refs[3 * npg:]
    s_id = pl.program_id(1)
    ones = jnp.ones((HEAD_DIM, LANES), BF16)
    qs = q_ref[...].astype(F32) * SCALE

    def lane_sums(x):
        return jnp.dot(x.astype(BF16), ones, preferred_element_type=F32)

    @pl.when(s_id == 0)
    def _():
        m_scr[...] = lane_sums(kn_ref[...].astype(F32) * qs)
        l_scr[...] = jnp.ones(l_scr.shape, F32)
        acc_scr[...] = vn_ref[...].astype(F32)
        tot_scr[...] = lfn_ref[...]
        lpad_scr[...] = jnp.zeros(lpad_scr.shape, F32)

    r = lax.broadcasted_iota(jnp.int32, (PAGE_SIZE, PAGE_SIZE), 0)
    c = lax.broadcasted_iota(jnp.int32, (PAGE_SIZE, PAGE_SIZE), 1)
    later = (c > r).astype(F32)

    tot = tot_scr[...]
    m_old = m_scr[...]
    m_new = m_old
    scores = []
    for j in range(npg):
        prod = (k_refs[j][...] * qs[None]).reshape(PAGE_SIZE * N_HEADS, HEAD_DIM)
        s = lane_sums(prod).reshape(PAGE_SIZE, N_HEADS, LANES)
        lpad_scr[:, 0:N_HEADS] = lp_refs[j][...]
        lp = lpad_scr[...]
        dec = jnp.dot(later, lp, precision=HIGHEST, preferred_element_type=F32) + tot
        tot = tot + jnp.sum(lp, axis=0, keepdims=True)
        dec_t = dec.T
        s = s + jnp.stack([jnp.broadcast_to(dec_t[0:N_HEADS, t:t + 1], (N_HEADS, LANES))
                           for t in range(PAGE_SIZE)], axis=0)
        scores.append(s)
        m_new = jnp.maximum(m_new, jnp.max(s, axis=0))
    tot_scr[...] = tot

    alpha = jnp.exp(m_old - m_new)
    l_new = alpha * l_scr[...]
    acc = alpha * acc_scr[...]
    for j, s in enumerate(scores):
        p = jnp.exp(s - m_new[None])
        l_new = l_new + jnp.sum(p, axis=0)
        acc = acc + jnp.sum(p * v_refs[j][...], axis=0)
    m_scr[...] = m_new
    l_scr[...] = l_new
    acc_scr[...] = acc

    @pl.when(s_id == n_steps - 1)
    def _():
        o_ref[...] = acc / l_new


def sample_attention(q, k_new, v_new, lf_new, cache_k, cache_v, cache_lf, page_table):
    db, n_pages = page_table.shape
    npg = PAGES_PER_STEP
    n_steps = n_pages // npg
    heads = pl.BlockSpec((None, N_HEADS, HEAD_DIM), lambda b, s, pt: (b, 0, 0))

    def page(j):
        return lambda b, s, pt: (pt[b, n_pages - 1 - (npg * s + j)], 0, 0, 0)

    def lf_page(j):
        return lambda b, s, pt: (pt[b, n_pages - 1 - (npg * s + j)], 0, 0)

    kv_specs = [pl.BlockSpec((None, PAGE_SIZE, N_HEADS, HEAD_DIM), page(j)) for j in range(npg)]
    lp_specs = [pl.BlockSpec((None, PAGE_SIZE, N_HEADS), lf_page(j)) for j in range(npg)]
    stat = pltpu.VMEM((N_HEADS, LANES), F32)
    grid_spec = pltpu.PrefetchScalarGridSpec(
        num_scalar_prefetch=1,
        grid=(db, n_steps),
        in_specs=[heads, heads, heads, pl.BlockSpec((None, 1, LANES), lambda b, s, pt: (b, 0, 0))]
                 + kv_specs + kv_specs + lp_specs,
        out_specs=heads,
        scratch_shapes=[pltpu.VMEM((PAGE_SIZE, LANES), F32), stat, stat,
                        pltpu.VMEM((1, LANES), F32), stat],
    )
    out = pl.pallas_call(
        functools.partial(_decode_kernel, n_steps=n_steps),
        grid_spec=grid_spec,
        out_shape=jax.ShapeDtypeStruct((db, N_HEADS, HEAD_DIM), F32),
        compiler_params=_params("parallel", "arbitrary"),
        name="sample_attention",
    )(page_table, q.reshape(db, N_HEADS, HEAD_DIM), k_new.reshape(db, N_HEADS, HEAD_DIM),
      v_new.reshape(db, N_HEADS, HEAD_DIM), lf_new.reshape(db, 1, LANES),
      *([cache_k] * npg), *([cache_v] * npg), *([cache_lf] * npg))
    return out.reshape(db, ATT_WIDTH)


def _mix_store(ao, conv, gb, gao_ref, gco_ref, o_ref):
    conv_o = gb * conv
    o_ref[:, 0:ATT_WIDTH] = ((ao * _row_rms_scale(ao)) * gao_ref[...]).astype(BF16)
    o_ref[:, ATT_WIDTH:D_MODEL] = ((conv_o * _row_rms_scale(conv_o)) * gco_ref[...]).astype(BF16)


def _mix_prompt_kernel(ao_ref, gb_ref, ci_ref, halo_ref, w_ref, gao_ref, gco_ref, o_ref, *,
                       tm, tiles_per_seq):
    ci = ci_ref[...]
    first = pl.program_id(0) % tiles_per_seq == 0
    halo = jnp.where(first, 0.0, halo_ref[...])
    row = lax.broadcasted_iota(jnp.int32, ci.shape, 0)
    prev1 = jnp.where(row == 0, halo[7:8, :], pltpu.roll(ci, 1, axis=0))
    prev2 = jnp.where(row == 0, halo[6:7, :],
                      jnp.where(row == 1, halo[7:8, :], pltpu.roll(ci, 2, axis=0)))
    conv = w_ref[0:1, :] * prev2 + w_ref[1:2, :] * prev1 + w_ref[2:3, :] * ci
    _mix_store(ao_ref[...], conv, gb_ref[...], gao_ref, gco_ref, o_ref)


def mix_prompt(ao, gb, ci, conv_w, g_ao, g_co, t, tm):
    m = ao.shape[0]
    rows = lambda i: (i, 0)
    fixed = lambda i: (0, 0)
    tile = pl.BlockSpec((tm, ATT_WIDTH), rows)
    halo = pl.BlockSpec((8, CONV_WIDTH), lambda i: (jnp.maximum(i * (tm // 8) - 1, 0), 0))
    return pl.pallas_call(
        functools.partial(_mix_prompt_kernel, tm=tm, tiles_per_seq=t // tm),
        grid=(m // tm,),
        in_specs=[tile, tile, tile, halo,
                  pl.BlockSpec((CONV_K, CONV_WIDTH), fixed),
                  pl.BlockSpec((1, ATT_WIDTH), fixed),
                  pl.BlockSpec((1, CONV_WIDTH), fixed)],
        out_specs=pl.BlockSpec((tm, D_MODEL), rows),
        out_shape=jax.ShapeDtypeStruct((m, D_MODEL), BF16),
        compiler_params=_params("parallel"),
        name="mix_prompt",
    )(ao, gb, ci, ci, conv_w, g_ao.reshape(1, -1), g_co.reshape(1, -1))


def _mix_sample_kernel(ao_ref, gb_ref, ci_ref, s0_ref, s1_ref, w_ref, gao_ref, gco_ref, o_ref):
    conv = w_ref[0:1, :] * s0_ref[...] + w_ref[1:2, :] * s1_ref[...] + w_ref[2:3, :] * ci_ref[...]
    _mix_store(ao_ref[...], conv, gb_ref[...], gao_ref, gco_ref, o_ref)


def mix_sample(ao, gb, ci, s0, s1, conv_w, g_ao, g_co):
    m = ao.shape[0]
    full = lambda shape: pl.BlockSpec(shape, lambda i: (0, 0))
    tile = full((m, ATT_WIDTH))
    return pl.pallas_call(
        _mix_sample_kernel,
        grid=(1,),
        in_specs=[tile] * 5 + [full((CONV_K, CONV_WIDTH)), full((1, ATT_WIDTH)), full((1, CONV_WIDTH))],
        out_specs=full((m, D_MODEL)),
        out_shape=jax.ShapeDtypeStruct((m, D_MODEL), BF16),
        compiler_params=_params("arbitrary"),
        name="mix_sample",
    )(ao, gb, ci, s0, s1, conv_w, g_ao.reshape(1, -1), g_co.reshape(1, -1))


def _row_chunks(tm):
    rc = min(tm, 256)
    return [slice(r * rc, (r + 1) * rc) for r in range(tm // rc)]


def _mm_resid_kernel(a_ref, w_ref, r_ref, o_ref, *, tm):
    for rows in _row_chunks(tm):
        o_ref[rows, :] = r_ref[rows, :] + jnp.dot(a_ref[rows, :], w_ref[...],
                                                  preferred_element_type=F32)


def matmul_residual(a, w, resid, tm, tn):
    m, k = a.shape
    n = w.shape[1]
    return pl.pallas_call(
        functools.partial(_mm_resid_kernel, tm=tm),
        grid=(m // tm, n // tn),
        in_specs=[pl.BlockSpec((tm, k), lambda i, j: (i, 0)),
                  pl.BlockSpec((k, tn), lambda i, j: (0, j)),
                  pl.BlockSpec((tm, tn), lambda i, j: (i, j))],
        out_specs=pl.BlockSpec((tm, tn), lambda i, j: (i, j)),
        out_shape=jax.ShapeDtypeStruct((m, n), F32),
        compiler_params=_params("parallel", "arbitrary"),
        name="matmul_residual",
    )(a, w, resid)


def _mm_relu2_kernel(a_ref, w_ref, o_ref, *, tm):
    for rows in _row_chunks(tm):
        z = jnp.maximum(jnp.dot(a_ref[rows, :], w_ref[...], preferred_element_type=F32), 0.0)
        o_ref[rows, :] = (z * z).astype(BF16)


def matmul_relu2(a, w, tm, tn):
    m, k = a.shape
    n = w.shape[1]
    return pl.pallas_call(
        functools.partial(_mm_relu2_kernel, tm=tm),
        grid=(m // tm, n // tn),
        in_specs=[pl.BlockSpec((tm, k), lambda i, j: (i, 0)),
                  pl.BlockSpec((k, tn), lambda i, j: (0, j))],
        out_specs=pl.BlockSpec((tm, tn), lambda i, j: (i, j)),
        out_shape=jax.ShapeDtypeStruct((m, n), BF16),
        compiler_params=_params("parallel", "arbitrary"),
        name="matmul_relu2",
    )(a, w)


def _mm_down_kernel(a_ref, w_ref, r_ref, o_ref, acc_ref, *, tm, nk):
    kk = pl.program_id(2)

    @pl.when(kk == 0)
    def _():
        acc_ref[...] = r_ref[...]

    for rows in _row_chunks(tm):
        acc_ref[rows, :] += jnp.dot(a_ref[rows, :], w_ref[...], preferred_element_type=F32)

    @pl.when(kk == nk - 1)
    def _():
        o_ref[...] = acc_ref[...]


def matmul_down_residual(a, w, resid, tm, tn, tk):
    m, k = a.shape
    n = w.shape[1]
    nk = k // tk
    return pl.pallas_call(
        functools.partial(_mm_down_kernel, tm=tm, nk=nk),
        grid=(m // tm, n // tn, nk),
        in_specs=[pl.BlockSpec((tm, tk), lambda i, j, kk: (i, kk)),
                  pl.BlockSpec((tk, tn), lambda i, j, kk: (kk, j)),
                  pl.BlockSpec((tm, tn), lambda i, j, kk: (i, j))],
        out_specs=pl.BlockSpec((tm, tn), lambda i, j, kk: (i, j)),
        out_shape=jax.ShapeDtypeStruct((m, n), F32),
        scratch_shapes=[pltpu.VMEM((tm, tn), F32)],
        compiler_params=_params("parallel", "parallel", "arbitrary"),
        name="matmul_down_residual",
    )(a, w, resid)


def _ple_kernel(u_ref, wg_ref, p_ref, we_ref, h_ref, o_ref, *, tm):
    for rows in _row_chunks(tm):
        gate = jax.nn.sigmoid(jnp.dot(u_ref[rows, :], wg_ref[...], preferred_element_type=F32))
        emb = jnp.dot(p_ref[rows, :], we_ref[...], preferred_element_type=F32)
        o_ref[rows, :] = h_ref[rows, :] + gate * emb


def gated_embedding(u, w_pg, p, w_pe, h, tm, tn):
    m, k = u.shape
    n = w_pg.shape[1]
    kp = p.shape[1]
    return pl.pallas_call(
        functools.partial(_ple_kernel, tm=tm),
        grid=(m // tm, n // tn),
        in_specs=[pl.BlockSpec((tm, k), lambda i, j: (i, 0)),
                  pl.BlockSpec((k, tn), lambda i, j: (0, j)),
                  pl.BlockSpec((tm, kp), lambda i, j: (i, 0)),
                  pl.BlockSpec((kp, tn), lambda i, j: (0, j)),
                  pl.BlockSpec((tm, tn), lambda i, j: (i, j))],
        out_specs=pl.BlockSpec((tm, tn), lambda i, j: (i, j)),
        out_shape=jax.ShapeDtypeStruct((m, n), F32),
        compiler_params=_params("parallel", "arbitrary"),
        name="gated_embedding",
    )(u, w_pg, p, w_pe, h)


def _dense_tail(x, o, p, wts, tm):
    w_out, g_mlp, w_up, w_down, g_ple, w_pg, w_pe = wts
    h = matmul_residual(o, w_out, x, tm, 512)
    u = rmsnorm_bf16(h, g_mlp, min(tm, 256))
    a = matmul_relu2(u, w_up, tm, 512)
    h = matmul_down_residual(a, w_down, h, tm, 1024, 2048)
    u = rmsnorm_bf16(h, g_ple, min(tm, 256))
    return gated_embedding(u, w_pg, p.astype(BF16), w_pe, h, tm, 512)


def kernel(x_prompt, x_sample, cache_k, cache_v, cache_logf, state_conv, page_table,
           p_prompt, p_sample, g_mix, w_in, b_f, q_gain, k_gain, conv_w,
           g_attn_out, g_conv_out, w_out, g_mlp, w_up, w_down, g_ple, w_pg, w_pe):
    assert w_in.shape[0] == 1, "single-layer trunk"
    batch, seq, _ = x_prompt.shape
    db = x_sample.shape[0]

    w = w_in[0].astype(BF16)
    att_end = 3 * ATT_WIDTH
    wa = w[:, :att_end].reshape(D_MODEL, 3, N_HEADS, HEAD_DIM)
    wc = w[:, att_end + N_HEADS:].reshape(D_MODEL, 3, N_HEADS, HEAD_DIM)
    w_heads = jnp.concatenate([wa, wc], axis=1).transpose(0, 2, 1, 3).reshape(D_MODEL, N_HEADS * HEAD_TILE)
    w_f = jnp.pad(w[:, att_end:att_end + N_HEADS], ((0, 0), (0, LANES - N_HEADS)))
    b_fp = jnp.pad(b_f[0], (0, LANES - N_HEADS)).reshape(1, LANES)
    tail_w = (w_out[0].astype(BF16), g_mlp[0], w_up[0].astype(BF16), w_down[0].astype(BF16),
              g_ple[0], w_pg[0].astype(BF16), w_pe[0].astype(BF16))

    xp = x_prompt.reshape(batch * seq, D_MODEL)
    u = rmsnorm_bf16(xp, g_mix[0], 256)
    q, kf, kb, vf, vb, gb, ci, lf = in_projection(u, w_heads, w_f, b_fp, q_gain[0], k_gain[0], 1024)
    c = cumsum_time(lf, batch, seq)
    ct = jnp.transpose(c[:, :N_HEADS].reshape(batch, seq, N_HEADS), (0, 2, 1))
    ao = prompt_attention(q, kb, vb, c, ct, batch, seq, 256)
    o = mix_prompt(ao, gb, ci, conv_w[0], g_attn_out[0], g_conv_out[0], seq, 256)
    yp = _dense_tail(xp, o, p_prompt[0].reshape(batch * seq, PLE_DIM), tail_w, 1024)

    k_prompt = kf.reshape(1, batch, seq, N_HEADS, HEAD_DIM)
    v_prompt = vf.reshape(1, batch, seq, N_HEADS, HEAD_DIM)
    logf_prompt = lf[:, :N_HEADS].reshape(1, batch, seq, N_HEADS)
    conv_prompt = ci.reshape(batch, seq, CONV_WIDTH)[None, :, seq - (CONV_K - 1):, :]

    xs = x_sample.reshape(db, D_MODEL)
    us = rmsnorm_bf16(xs, g_mix[0], db)
    qs, kfs, kbs, vfs, vbs, gbs, cis, lfs = in_projection(
        us, w_heads, w_f, b_fp, q_gain[0], k_gain[0], db)
    aos = sample_attention(qs, kbs, vbs, lfs, cache_k[0], cache_v[0], cache_logf[0], page_table)
    st = state_conv[0]
    os_ = mix_sample(aos, gbs, cis, st[:, 0, :], st[:, 1, :], conv_w[0], g_attn_out[0], g_conv_out[0])
    ys = _dense_tail(xs, os_, p_sample[0].reshape(db, PLE_DIM), tail_w, db)

    k_sample = kfs.reshape(1, db, 1, N_HEADS, HEAD_DIM)
    v_sample = vfs.reshape(1, db, 1, N_HEADS, HEAD_DIM)
    logf_sample = lfs[:, :N_HEADS].reshape(1, db, 1, N_HEADS)
    conv_sample = jnp.stack([st[:, 1, :], cis], axis=1)[None]

    return (yp.reshape(batch, seq, D_MODEL), ys.reshape(db, 1, D_MODEL),
            k_prompt, v_prompt, logf_prompt, conv_prompt,
            k_sample, v_sample, logf_sample, conv_sample)
```

```python
import functools
import math

import jax
import jax.numpy as jnp
from jax import lax
from jax.experimental import pallas as pl
from jax.experimental.pallas import tpu as pltpu

F32 = jnp.float32
BF16 = jnp.bfloat16

D_MODEL = 4096
HEAD_DIM = 128
N_HEADS = 16
ATT_WIDTH = N_HEADS * HEAD_DIM
CONV_WIDTH = D_MODEL - ATT_WIDTH
CONV_K = 3
PLE_DIM = 256
PAGE_SIZE = 128
EPS = 1e-6
NEG_INF = -1e30
LOG2E = math.log2(math.e)
Q_SCALE = HEAD_DIM ** -0.5 * LOG2E
LANES = 128
HEAD_TILE = 6 * HEAD_DIM
VMEM_LIMIT = 56 * 1024 * 1024
HIGHEST = lax.Precision.HIGHEST
N_SPLIT = 3


def _params(*sem):
    return pltpu.CompilerParams(dimension_semantics=sem, vmem_limit_bytes=VMEM_LIMIT)


def _log_sigmoid(x):
    return jnp.minimum(x, 0.0) - jnp.log1p(jnp.exp(-jnp.abs(x)))


def _row_rms_scale(x):
    return lax.rsqrt(jnp.mean(x * x, axis=-1, keepdims=True) + EPS)


def _split3(x):
    hi = x.astype(BF16).astype(F32)
    r = x - hi
    mid = r.astype(BF16).astype(F32)
    lo = (r - mid).astype(BF16).astype(F32)
    return hi, mid, lo


def _rmsnorm_kernel(x_ref, g_ref, o_ref):
    x = x_ref[...]
    o_ref[...] = ((x * _row_rms_scale(x)) * g_ref[...]).astype(o_ref.dtype)


def rmsnorm_bf16(x, g, tm):
    m, d = x.shape
    return pl.pallas_call(
        _rmsnorm_kernel,
        grid=(m // tm,),
        in_specs=[pl.BlockSpec((tm, d), lambda i: (i, 0)),
                  pl.BlockSpec((1, d), lambda i: (0, 0))],
        out_specs=pl.BlockSpec((tm, d), lambda i: (i, 0)),
        out_shape=jax.ShapeDtypeStruct((m, d), BF16),
        compiler_params=_params("parallel"),
        name="rmsnorm_bf16",
    )(x, g.reshape(1, d))


def _rmsnorm_forget_kernel(x_ref, g_ref, wf_ref, bf_ref, u_ref, lf_ref):
    x = x_ref[...]
    u = ((x * _row_rms_scale(x)) * g_ref[...]).astype(BF16)
    u_ref[...] = u
    f = jnp.dot(u, wf_ref[...], preferred_element_type=F32) + bf_ref[...]
    lf_ref[...] = _log_sigmoid(f)


def rmsnorm_forget(x, g, w_f, b_f, tm):
    m, d = x.shape
    fixed = lambda i: (0, 0)
    return pl.pallas_call(
        _rmsnorm_forget_kernel,
        grid=(m // tm,),
        in_specs=[pl.BlockSpec((tm, d), lambda i: (i, 0)),
                  pl.BlockSpec((1, d), fixed),
                  pl.BlockSpec((d, LANES), fixed),
                  pl.BlockSpec((1, LANES), fixed)],
        out_specs=[pl.BlockSpec((tm, d), lambda i: (i, 0)),
                   pl.BlockSpec((tm, LANES), lambda i: (i, 0))],
        out_shape=[jax.ShapeDtypeStruct((m, d), BF16), jax.ShapeDtypeStruct((m, LANES), F32)],
        compiler_params=_params("parallel"),
        name="rmsnorm_forget",
    )(x, g.reshape(1, d), w_f, b_f)


def _inproj_kernel(u_ref, wq_ref, wk_ref, wv_ref, wb_ref, wc_ref, wh_ref, qg_ref, kg_ref,
                   q_ref, kf_ref, kb_ref, vf_ref, vb_ref, gb_ref, ci_ref, w_scr, *, tm, rc):
    d = HEAD_DIM

    @pl.when(pl.program_id(1) == 0)
    def _():
        for s, w_ref in enumerate((wq_ref, wk_ref, wv_ref, wb_ref, wc_ref, wh_ref)):
            w_scr[:, s * d:(s + 1) * d] = w_ref[...].astype(BF16)

    for r in range(tm // rc):
        rows = slice(r * rc, (r + 1) * rc)
        z = jnp.dot(u_ref[rows, :], w_scr[...], preferred_element_type=F32)
        zq, zk, zv = z[:, 0:d], z[:, d:2 * d], z[:, 2 * d:3 * d]
        q = (zq * _row_rms_scale(zq)) * qg_ref[...]
        k = (zk * _row_rms_scale(zk)) * kg_ref[...]
        q_ref[rows, :] = (q * Q_SCALE).astype(BF16)
        kf_ref[rows, :] = k
        kb_ref[rows, :] = k.astype(BF16)
        vf_ref[rows, :] = zv
        vb_ref[rows, :] = zv.astype(BF16)
        gb_ref[rows, :] = z[:, 3 * d:4 * d]
        ci_ref[rows, :] = z[:, 4 * d:5 * d] * z[:, 5 * d:6 * d]


def in_projection(u, w_in, w_conv, q_gain, k_gain, tm):
    m, dm = u.shape
    rc = min(tm, 256)
    nh = N_HEADS
    fixed = lambda h, i: (0, 0)
    col = lambda s: pl.BlockSpec((dm, HEAD_DIM), lambda h, i: (0, s * nh + h))
    head_out = pl.BlockSpec((tm, HEAD_DIM), lambda h, i: (i, h))
    shapes = [jax.ShapeDtypeStruct((m, ATT_WIDTH), dt)
              for dt in (BF16, F32, BF16, F32, BF16, F32, F32)]
    return pl.pallas_call(
        functools.partial(_inproj_kernel, tm=tm, rc=rc),
        grid=(nh, m // tm),
        in_specs=[pl.BlockSpec((tm, dm), lambda h, i: (i, 0)),
                  col(0), col(1), col(2), col(0), col(1), col(2),
                  pl.BlockSpec((1, HEAD_DIM), fixed),
                  pl.BlockSpec((1, HEAD_DIM), fixed)],
        out_specs=[head_out] * 7,
        out_shape=shapes,
        scratch_shapes=[pltpu.VMEM((dm, HEAD_TILE), BF16)],
        compiler_params=_params("parallel", "arbitrary"),
        name="in_projection",
    )(u, w_in, w_in, w_in, w_conv, w_conv, w_conv,
      q_gain.reshape(1, HEAD_DIM), k_gain.reshape(1, HEAD_DIM))


def _decay_features_kernel(lf_ref, qx_ref, kx_ref, *, t):
    r = lax.broadcasted_iota(jnp.int32, (LANES, LANES), 0)
    c = lax.broadcasted_iota(jnp.int32, (LANES, LANES), 1)
    tri = (c <= r).astype(F32)
    lane = c
    carry = jnp.zeros((1, LANES), F32)
    for j in range(t // LANES):
        rows = slice(j * LANES, (j + 1) * LANES)
        cs = jnp.dot(tri, lf_ref[rows, :], precision=HIGHEST, preferred_element_type=F32) + carry
        carry = cs[LANES - 1:LANES, :]
        pieces = _split3(cs * LOG2E)
        for h in range(N_HEADS):
            cols = slice(h * HEAD_DIM, (h + 1) * HEAD_DIM)
            qx = jnp.where((lane >= N_SPLIT) & (lane < 2 * N_SPLIT), 1.0, 0.0)
            kx = jnp.where(lane < N_SPLIT, 1.0, 0.0)
            for s, piece in enumerate(pieces):
                col = piece[:, h:h + 1]
                qx = jnp.where(lane == s, col, qx)
                kx = jnp.where(lane == N_SPLIT + s, -col, kx)
            qx_ref[rows, cols] = qx.astype(BF16)
            kx_ref[rows, cols] = kx.astype(BF16)


def decay_features(lf, batch, t):
    m = lf.shape[0]
    out = pl.BlockSpec((t, ATT_WIDTH), lambda b: (b, 0))
    return pl.pallas_call(
        functools.partial(_decay_features_kernel, t=t),
        grid=(batch,),
        in_specs=[pl.BlockSpec((t, LANES), lambda b: (b, 0))],
        out_specs=[out, out],
        out_shape=[jax.ShapeDtypeStruct((m, ATT_WIDTH), BF16)] * 2,
        compiler_params=_params("parallel"),
        name="decay_features",
    )(lf)


def _flash_kernel(q_ref, qx_ref, k_ref, kx_ref, v_ref, o_ref, m_scr, l_scr, acc_scr, *, tq, tk):
    qi = pl.program_id(1)
    kj = pl.program_id(2)
    d = HEAD_DIM

    @pl.when(kj == 0)
    def _():
        m_scr[...] = jnp.full(m_scr.shape, NEG_INF, F32)
        l_scr[...] = jnp.zeros(l_scr.shape, F32)
        acc_scr[...] = jnp.zeros(acc_scr.shape, F32)

    def step(masked):
        if masked:
            row = lax.broadcasted_iota(jnp.int32, (tq, tk), 0)
            col = lax.broadcasted_iota(jnp.int32, (tq, tk), 1)
            keep = col <= row
        for h in range(N_HEADS):
            cs = slice(h * d, (h + 1) * d)
            qa = jnp.concatenate([q_ref[:, cs], qx_ref[:, cs]], axis=1)
            ka = jnp.concatenate([k_ref[:, cs], kx_ref[:, cs]], axis=1)
            s = lax.dot_general(qa, ka, (((1,), (1,)), ((), ())), preferred_element_type=F32)
            if masked:
                s = jnp.where(keep, s, NEG_INF)
            m_old = m_scr[h]
            m_new = jnp.maximum(m_old, jnp.max(s, axis=-1, keepdims=True))
            p = jnp.exp2(s - pltpu.repeat(m_new, tk // LANES, axis=1))
            alpha = jnp.exp2(m_old - m_new)
            l_scr[h] = alpha * l_scr[h] + jnp.sum(p, axis=-1, keepdims=True)
            m_scr[h] = m_new
            acc_scr[:, cs] = alpha * acc_scr[:, cs] + jnp.dot(
                p.astype(BF16), v_ref[:, cs], preferred_element_type=F32)

    @pl.when(kj < qi)
    def _():
        step(False)

    @pl.when(kj == qi)
    def _():
        step(True)
        for h in range(N_HEADS):
            cs = slice(h * d, (h + 1) * d)
            o_ref[:, cs] = acc_scr[:, cs] / l_scr[h]


def prompt_attention(q, qx, k, kx, v, batch, t, tq):
    m = q.shape[0]
    nq = t // tq
    qrow = lambda b, i, j: (b * nq + i, 0)
    krow = lambda b, i, j: (b * nq + jnp.minimum(i, j), 0)
    qspec = pl.BlockSpec((tq, ATT_WIDTH), qrow)
    kspec = pl.BlockSpec((tq, ATT_WIDTH), krow)
    return pl.pallas_call(
        functools.partial(_flash_kernel, tq=tq, tk=tq),
        grid=(batch, nq, nq),
        in_specs=[qspec, qspec, kspec, kspec, kspec],
        out_specs=qspec,
        out_shape=jax.ShapeDtypeStruct((m, ATT_WIDTH), F32),
        scratch_shapes=[pltpu.VMEM((N_HEADS, tq, LANES), F32),
                        pltpu.VMEM((N_HEADS, tq, LANES), F32),
                        pltpu.VMEM((tq, ATT_WIDTH), F32)],
        compiler_params=_params("parallel", "parallel", "arbitrary"),
        name="prompt_attention",
    )(q, qx, k, kx, v)


PAGES_PER_STEP = 4


def _decode_kernel(pt_ref, q_ref, kn_ref, vn_ref, lfn_ref, *refs, n_steps):
    del pt_ref
    npg = PAGES_PER_STEP
    k_refs, v_refs, lp_refs = refs[0:npg], refs[npg:2 * npg], refs[2 * npg:3 * npg]
    o_ref, lpad_scr, m_scr, l_scr, tot_scr, acc_scr = refs[3 * npg:]
    s_id = pl.program_id(1)
    ones = jnp.ones((2 * HEAD_DIM, LANES), BF16)
    qs = q_ref[...].astype(F32)

    @pl.when(s_id == 0)
    def _():
        prod = (kn_ref[...].astype(F32) * qs).astype(BF16)
        m_scr[...] = jnp.dot(prod, ones[0:HEAD_DIM, :], preferred_element_type=F32)
        l_scr[...] = jnp.ones(l_scr.shape, F32)
        acc_scr[...] = vn_ref[...].astype(F32)
        tot_scr[...] = lfn_ref[...]
        lpad_scr[...] = jnp.zeros(lpad_scr.shape, F32)

    r = lax.broadcasted_iota(jnp.int32, (PAGE_SIZE, PAGE_SIZE), 0)
    c = lax.broadcasted_iota(jnp.int32, (PAGE_SIZE, PAGE_SIZE), 1)
    later = (c > r).astype(F32)
    head_lane = c < N_HEADS
    hh = lax.broadcasted_iota(jnp.int32, (N_HEADS, LANES), 0)
    ll = lax.broadcasted_iota(jnp.int32, (N_HEADS, LANES), 1)
    own = ((ll % N_HEADS) == hh) & (ll < N_SPLIT * N_HEADS)

    tot = tot_scr[...]
    m_old = m_scr[...]
    m_new = m_old
    scores = []
    for j in range(npg):
        prod = (k_refs[j][...] * qs[None]).astype(BF16).reshape(PAGE_SIZE * N_HEADS, HEAD_DIM)
        lpad_scr[:, 0:N_HEADS] = lp_refs[j][...]
        lp = lpad_scr[...]
        dec = jnp.dot(later, lp, precision=HIGHEST, preferred_element_type=F32) + tot
        tot = tot + jnp.sum(lp, axis=0, keepdims=True)
        hi, mid, lo = _split3(jnp.where(head_lane, dec * LOG2E, 0.0))
        pieces = hi + pltpu.roll(mid, N_HEADS, axis=1) + pltpu.roll(lo, 2 * N_HEADS, axis=1)
        spread = jnp.broadcast_to(pieces[:, None, :], (PAGE_SIZE, N_HEADS, LANES))
        extra = jnp.where(own[None], spread, 0.0).astype(BF16).reshape(PAGE_SIZE * N_HEADS, LANES)
        s = jnp.dot(jnp.concatenate([prod, extra], axis=1), ones, preferred_element_type=F32)
        s = s.reshape(PAGE_SIZE, N_HEADS, LANES)
        scores.append(s)
        m_new = jnp.maximum(m_new, jnp.max(s, axis=0))
    tot_scr[...] = tot

    alpha = jnp.exp2(m_old - m_new)
    l_new = alpha * l_scr[...]
    acc = alpha * acc_scr[...]
    for j, s in enumerate(scores):
        p = jnp.exp2(s - m_new[None])
        l_new = l_new + jnp.sum(p, axis=0)
        acc = acc + jnp.sum(p * v_refs[j][...], axis=0)
    m_scr[...] = m_new
    l_scr[...] = l_new
    acc_scr[...] = acc

    @pl.when(s_id == n_steps - 1)
    def _():
        o_ref[...] = acc / l_new


def sample_attention(q, k_new, v_new, lf_new, cache_k, cache_v, cache_lf, page_table):
    db, n_pages = page_table.shape
    npg = PAGES_PER_STEP
    n_steps = n_pages // npg
    heads = pl.BlockSpec((None, N_HEADS, HEAD_DIM), lambda b, s, pt: (b, 0, 0))

    def page(j):
        return lambda b, s, pt: (pt[b, n_pages - 1 - (npg * s + j)], 0, 0, 0)

    def lf_page(j):
        return lambda b, s, pt: (pt[b, n_pages - 1 - (npg * s + j)], 0, 0)

    kv_specs = [pl.BlockSpec((None, PAGE_SIZE, N_HEADS, HEAD_DIM), page(j)) for j in range(npg)]
    lp_specs = [pl.BlockSpec((None, PAGE_SIZE, N_HEADS), lf_page(j)) for j in range(npg)]
    stat = pltpu.VMEM((N_HEADS, LANES), F32)
    grid_spec = pltpu.PrefetchScalarGridSpec(
        num_scalar_prefetch=1,
        grid=(db, n_steps),
        in_specs=[heads, heads, heads, pl.BlockSpec((None, 1, LANES), lambda b, s, pt: (b, 0, 0))]
                 + kv_specs + kv_specs + lp_specs,
        out_specs=heads,
        scratch_shapes=[pltpu.VMEM((PAGE_SIZE, LANES), F32), stat, stat,
                        pltpu.VMEM((1, LANES), F32), stat],
    )
    out = pl.pallas_call(
        functools.partial(_decode_kernel, n_steps=n_steps),
        grid_spec=grid_spec,
        out_shape=jax.ShapeDtypeStruct((db, N_HEADS, HEAD_DIM), F32),
        compiler_params=_params("parallel", "arbitrary"),
        name="sample_attention",
    )(page_table, q.reshape(db, N_HEADS, HEAD_DIM), k_new.reshape(db, N_HEADS, HEAD_DIM),
      v_new.reshape(db, N_HEADS, HEAD_DIM), lf_new.reshape(db, 1, LANES),
      *([cache_k] * npg), *([cache_v] * npg), *([cache_lf] * npg))
    return out.reshape(db, ATT_WIDTH)


def _mix_store(ao, conv, gb, gao_ref, gco_ref, o_ref):
    conv_o = gb * conv
    o_ref[:, 0:ATT_WIDTH] = ((ao * _row_rms_scale(ao)) * gao_ref[...]).astype(BF16)
    o_ref[:, ATT_WIDTH:D_MODEL] = ((conv_o * _row_rms_scale(conv_o)) * gco_ref[...]).astype(BF16)


def _mix_prompt_kernel(ao_ref, gb_ref, ci_ref, halo_ref, w_ref, gao_ref, gco_ref, o_ref, *,
                       tm, tiles_per_seq):
    ci = ci_ref[...]
    first = pl.program_id(0) % tiles_per_seq == 0
    halo = jnp.where(first, 0.0, halo_ref[...])
    row = lax.broadcasted_iota(jnp.int32, ci.shape, 0)
    prev1 = jnp.where(row == 0, halo[7:8, :], pltpu.roll(ci, 1, axis=0))
    prev2 = jnp.where(row == 0, halo[6:7, :],
                      jnp.where(row == 1, halo[7:8, :], pltpu.roll(ci, 2, axis=0)))
    conv = w_ref[0:1, :] * prev2 + w_ref[1:2, :] * prev1 + w_ref[2:3, :] * ci
    _mix_store(ao_ref[...], conv, gb_ref[...], gao_ref, gco_ref, o_ref)


def mix_prompt(ao, gb, ci, conv_w, g_ao, g_co, t, tm):
    m = ao.shape[0]
    rows = lambda i: (i, 0)
    fixed = lambda i: (0, 0)
    tile = pl.BlockSpec((tm, ATT_WIDTH), rows)
    halo = pl.BlockSpec((8, CONV_WIDTH), lambda i: (jnp.maximum(i * (tm // 8) - 1, 0), 0))
    return pl.pallas_call(
        functools.partial(_mix_prompt_kernel, tm=tm, tiles_per_seq=t // tm),
        grid=(m // tm,),
        in_specs=[tile, tile, tile, halo,
                  pl.BlockSpec((CONV_K, CONV_WIDTH), fixed),
                  pl.BlockSpec((1, ATT_WIDTH), fixed),
                  pl.BlockSpec((1, CONV_WIDTH), fixed)],
        out_specs=pl.BlockSpec((tm, D_MODEL), rows),
        out_shape=jax.ShapeDtypeStruct((m, D_MODEL), BF16),
        compiler_params=_params("parallel"),
        name="mix_prompt",
    )(ao, gb, ci, ci, conv_w, g_ao.reshape(1, -1), g_co.reshape(1, -1))


def _mix_sample_kernel(ao_ref, gb_ref, ci_ref, s0_ref, s1_ref, w_ref, gao_ref, gco_ref, o_ref):
    conv = w_ref[0:1, :] * s0_ref[...] + w_ref[1:2, :] * s1_ref[...] + w_ref[2:3, :] * ci_ref[...]
    _mix_store(ao_ref[...], conv, gb_ref[...], gao_ref, gco_ref, o_ref)


def mix_sample(ao, gb, ci, s0, s1, conv_w, g_ao, g_co):
    m = ao.shape[0]
    full = lambda shape: pl.BlockSpec(shape, lambda i: (0, 0))
    tile = full((m, ATT_WIDTH))
    return pl.pallas_call(
        _mix_sample_kernel,
        grid=(1,),
        in_specs=[tile] * 5 + [full((CONV_K, CONV_WIDTH)), full((1, ATT_WIDTH)), full((1, CONV_WIDTH))],
        out_specs=full((m, D_MODEL)),
        out_shape=jax.ShapeDtypeStruct((m, D_MODEL), BF16),
        compiler_params=_params("arbitrary"),
        name="mix_sample",
    )(ao, gb, ci, s0, s1, conv_w, g_ao.reshape(1, -1), g_co.reshape(1, -1))


def _row_chunks(tm):
    rc = min(tm, 256)
    return [slice(r * rc, (r + 1) * rc) for r in range(tm // rc)]


def _mm_resid_kernel(a_ref, w_ref, r_ref, o_ref, w_scr, *, tm):
    w_scr[...] = w_ref[...].astype(BF16)
    for rows in _row_chunks(tm):
        o_ref[rows, :] = r_ref[rows, :] + jnp.dot(a_ref[rows, :], w_scr[...],
                                                  preferred_element_type=F32)


def matmul_residual(a, w, resid, tm, tn):
    m, k = a.shape
    n = w.shape[1]
    return pl.pallas_call(
        functools.partial(_mm_resid_kernel, tm=tm),
        grid=(m // tm, n // tn),
        in_specs=[pl.BlockSpec((tm, k), lambda i, j: (i, 0)),
                  pl.BlockSpec((k, tn), lambda i, j: (0, j)),
                  pl.BlockSpec((tm, tn), lambda i, j: (i, j))],
        out_specs=pl.BlockSpec((tm, tn), lambda i, j: (i, j)),
        out_shape=jax.ShapeDtypeStruct((m, n), F32),
        scratch_shapes=[pltpu.VMEM((k, tn), BF16)],
        compiler_params=_params("parallel", "arbitrary"),
        name="matmul_residual",
    )(a, w, resid)


def _mm_relu2_kernel(a_ref, w_ref, o_ref, w_scr, *, tm):
    w_scr[...] = w_ref[...].astype(BF16)
    for rows in _row_chunks(tm):
        z = jnp.maximum(jnp.dot(a_ref[rows, :], w_scr[...], preferred_element_type=F32), 0.0)
        o_ref[rows, :] = (z * z).astype(BF16)


def matmul_relu2(a, w, tm, tn):
    m, k = a.shape
    n = w.shape[1]
    return pl.pallas_call(
        functools.partial(_mm_relu2_kernel, tm=tm),
        grid=(m // tm, n // tn),
        in_specs=[pl.BlockSpec((tm, k), lambda i, j: (i, 0)),
                  pl.BlockSpec((k, tn), lambda i, j: (0, j))],
        out_specs=pl.BlockSpec((tm, tn), lambda i, j: (i, j)),
        out_shape=jax.ShapeDtypeStruct((m, n), BF16),
        scratch_shapes=[pltpu.VMEM((k, tn), BF16)],
        compiler_params=_params("parallel", "arbitrary"),
        name="matmul_relu2",
    )(a, w)


def _mm_down_kernel(a_ref, w_ref, r_ref, o_ref, acc_ref, w_scr, *, tm, nk):
    kk = pl.program_id(2)
    w_scr[...] = w_ref[...].astype(BF16)

    @pl.when(kk == 0)
    def _():
        acc_ref[...] = r_ref[...]

    for rows in _row_chunks(tm):
        acc_ref[rows, :] += jnp.dot(a_ref[rows, :], w_scr[...], preferred_element_type=F32)

    @pl.when(kk == nk - 1)
    def _():
        o_ref[...] = acc_ref[...]


def matmul_down_residual(a, w, resid, tm, tn, tk):
    m, k = a.shape
    n = w.shape[1]
    nk = k // tk
    return pl.pallas_call(
        functools.partial(_mm_down_kernel, tm=tm, nk=nk),
        grid=(m // tm, n // tn, nk),
        in_specs=[pl.BlockSpec((tm, tk), lambda i, j, kk: (i, kk)),
                  pl.BlockSpec((tk, tn), lambda i, j, kk: (kk, j)),
                  pl.BlockSpec((tm, tn), lambda i, j, kk: (i, j))],
        out_specs=pl.BlockSpec((tm, tn), lambda i, j, kk: (i, j)),
        out_shape=jax.ShapeDtypeStruct((m, n), F32),
        scratch_shapes=[pltpu.VMEM((tm, tn), F32), pltpu.VMEM((tk, tn), BF16)],
        compiler_params=_params("parallel", "parallel", "arbitrary"),
        name="matmul_down_residual",
    )(a, w, resid)


def _ple_kernel(u_ref, wg_ref, p_ref, we_ref, h_ref, o_ref, w_scr, *, tm):
    w_scr[...] = wg_ref[...].astype(BF16)
    we = we_ref[...].astype(BF16)
    for rows in _row_chunks(tm):
        gate = jax.nn.sigmoid(jnp.dot(u_ref[rows, :], w_scr[...], preferred_element_type=F32))
        emb = jnp.dot(p_ref[rows, :].astype(BF16), we, preferred_element_type=F32)
        o_ref[rows, :] = h_ref[rows, :] + gate * emb


def gated_embedding(u, w_pg, p, w_pe, h, tm, tn):
    m, k = u.shape
    n = w_pg.shape[1]
    kp = p.shape[1]
    return pl.pallas_call(
        functools.partial(_ple_kernel, tm=tm),
        grid=(m // tm, n // tn),
        in_specs=[pl.BlockSpec((tm, k), lambda i, j: (i, 0)),
                  pl.BlockSpec((k, tn), lambda i, j: (0, j)),
                  pl.BlockSpec((tm, kp), lambda i, j: (i, 0)),
                  pl.BlockSpec((kp, tn), lambda i, j: (0, j)),
                  pl.BlockSpec((tm, tn), lambda i, j: (i, j))],
        out_specs=pl.BlockSpec((tm, tn), lambda i, j: (i, j)),
        out_shape=jax.ShapeDtypeStruct((m, n), F32),
        scratch_shapes=[pltpu.VMEM((k, tn), BF16)],
        compiler_params=_params("parallel", "arbitrary"),
        name="gated_embedding",
    )(u, w_pg, p, w_pe, h)


def _dense_tail(x, o, p, wts, tm):
    w_out, g_mlp, w_up, w_down, g_ple, w_pg, w_pe = wts
    h = matmul_residual(o, w_out, x, tm, 512)
    u = rmsnorm_bf16(h, g_mlp, min(tm, 512))
    a = matmul_relu2(u, w_up, tm, 512)
    h = matmul_down_residual(a, w_down, h, tm, 1024, 2048)
    u = rmsnorm_bf16(h, g_ple, min(tm, 512))
    return gated_embedding(u, w_pg, p, w_pe, h, tm, 512)


def kernel(x_prompt, x_sample, cache_k, cache_v, cache_logf, state_conv, page_table,
           p_prompt, p_sample, g_mix, w_in, b_f, q_gain, k_gain, conv_w,
           g_attn_out, g_conv_out, w_out, g_mlp, w_up, w_down, g_ple, w_pg, w_pe):
    assert w_in.shape[0] == 1, "single-layer trunk"
    batch, seq, _ = x_prompt.shape
    db = x_sample.shape[0]

    att_end = 3 * ATT_WIDTH
    w_conv = w_in[0][:, att_end + N_HEADS:]
    w_f = jnp.pad(w_in[0][:, att_end:att_end + N_HEADS], ((0, 0), (0, LANES - N_HEADS))).astype(BF16)
    b_fp = jnp.pad(b_f[0], (0, LANES - N_HEADS)).reshape(1, LANES)
    tail_w = (w_out[0], g_mlp[0], w_up[0], w_down[0], g_ple[0], w_pg[0], w_pe[0])

    xp = x_prompt.reshape(batch * seq, D_MODEL)
    u, lf = rmsnorm_forget(xp, g_mix[0], w_f, b_fp, 512)
    q, kf, kb, vf, vb, gb, ci = in_projection(u, w_in[0], w_conv, q_gain[0], k_gain[0], 512)
    qx, kx = decay_features(lf, batch, seq)
    ao = prompt_attention(q, qx, kb, kx, vb, batch, seq, 256)
    o = mix_prompt(ao, gb, ci, conv_w[0], g_attn_out[0], g_conv_out[0], seq, 256)
    yp = _dense_tail(xp, o, p_prompt[0].reshape(batch * seq, PLE_DIM), tail_w, 1024)

    k_prompt = kf.reshape(1, batch, seq, N_HEADS, HEAD_DIM)
    v_prompt = vf.reshape(1, batch, seq, N_HEADS, HEAD_DIM)
    logf_prompt = lf[:, :N_HEADS].reshape(1, batch, seq, N_HEADS)
    conv_prompt = ci.reshape(batch, seq, CONV_WIDTH)[None, :, seq - (CONV_K - 1):, :]

    xs = x_sample.reshape(db, D_MODEL)
    us, lfs = rmsnorm_forget(xs, g_mix[0], w_f, b_fp, db)
    qs, kfs, kbs, vfs, vbs, gbs, cis = in_projection(us, w_in[0], w_conv, q_gain[0], k_gain[0], db)
    aos = sample_attention(qs, kbs, vbs, lfs, cache_k[0], cache_v[0], cache_logf[0], page_table)
    st = state_conv[0]
    os_ = mix_sample(aos, gbs, cis, st[:, 0, :], st[:, 1, :], conv_w[0], g_attn_out[0], g_conv_out[0])
    ys = _dense_tail(xs, os_, p_sample[0].reshape(db, PLE_DIM), tail_w, db)

    k_sample = kfs.reshape(1, db, 1, N_HEADS, HEAD_DIM)
    v_sample = vfs.reshape(1, db, 1, N_HEADS, HEAD_DIM)
    logf_sample = lfs[:, :N_HEADS].reshape(1, db, 1, N_HEADS)
    conv_sample = jnp.stack([st[:, 1, :], cis], axis=1)[None]

    return (yp.reshape(batch, seq, D_MODEL), ys.reshape(db, 1, D_MODEL),
            k_prompt, v_prompt, logf_prompt, conv_prompt,
            k_sample, v_sample, logf_sample, conv_sample)
```

```python
import functools
import math

import jax
import jax.numpy as jnp
from jax import lax
from jax.experimental import pallas as pl
from jax.experimental.pallas import tpu as pltpu

F32 = jnp.float32
BF16 = jnp.bfloat16

D_MODEL = 4096
HEAD_DIM = 128
N_HEADS = 16
ATT_WIDTH = N_HEADS * HEAD_DIM
CONV_WIDTH = D_MODEL - ATT_WIDTH
CONV_K = 3
PLE_DIM = 256
PAGE_SIZE = 128
EPS = 1e-6
NEG_INF = -1e30
LOG2E = math.log2(math.e)
Q_SCALE = HEAD_DIM ** -0.5 * LOG2E
LANES = 128
HEAD_TILE = 6 * HEAD_DIM
VMEM_LIMIT = 56 * 1024 * 1024
HIGHEST = lax.Precision.HIGHEST
N_SPLIT = 3


def _params(*sem):
    return pltpu.CompilerParams(dimension_semantics=sem, vmem_limit_bytes=VMEM_LIMIT)


def _log_sigmoid(x):
    return jnp.minimum(x, 0.0) - jnp.log1p(jnp.exp(-jnp.abs(x)))


def _row_rms_scale(x):
    return lax.rsqrt(jnp.mean(x * x, axis=-1, keepdims=True) + EPS)


def _split3(x):
    hi = x.astype(BF16).astype(F32)
    r = x - hi
    mid = r.astype(BF16).astype(F32)
    lo = (r - mid).astype(BF16).astype(F32)
    return hi, mid, lo


def _dot(a, b):
    return jnp.dot(a, b, preferred_element_type=F32)


def _dot_nt(a, b):
    return lax.dot_general(a, b, (((1,), (1,)), ((), ())), preferred_element_type=F32)


def _rmsnorm_kernel(x_ref, g_ref, o_ref):
    x = x_ref[...]
    o_ref[...] = ((x * _row_rms_scale(x)) * g_ref[...]).astype(o_ref.dtype)


def rmsnorm_bf16(x, g, tm):
    m, d = x.shape
    return pl.pallas_call(
        _rmsnorm_kernel,
        grid=(m // tm,),
        in_specs=[pl.BlockSpec((tm, d), lambda i: (i, 0)),
                  pl.BlockSpec((1, d), lambda i: (0, 0))],
        out_specs=pl.BlockSpec((tm, d), lambda i: (i, 0)),
        out_shape=jax.ShapeDtypeStruct((m, d), BF16),
        compiler_params=_params("parallel"),
        name="rmsnorm_bf16",
    )(x, g.reshape(1, d))


def _rmsnorm_forget_kernel(x_ref, g_ref, wf_ref, bf_ref, u_ref, lf_ref):
    x = x_ref[...]
    u = ((x * _row_rms_scale(x)) * g_ref[...]).astype(BF16)
    u_ref[...] = u
    lf_ref[...] = _log_sigmoid(_dot_nt(u, wf_ref[...].astype(BF16)) + bf_ref[...])


def rmsnorm_forget(x, g, w_in_t, b_f, tm):
    m, d = x.shape
    fixed = lambda i: (0, 0)
    f_block = 3 * ATT_WIDTH // LANES
    return pl.pallas_call(
        _rmsnorm_forget_kernel,
        grid=(m // tm,),
        in_specs=[pl.BlockSpec((tm, d), lambda i: (i, 0)),
                  pl.BlockSpec((1, d), fixed),
                  pl.BlockSpec((LANES, d), lambda i: (f_block, 0)),
                  pl.BlockSpec((1, LANES), fixed)],
        out_specs=[pl.BlockSpec((tm, d), lambda i: (i, 0)),
                   pl.BlockSpec((tm, LANES), lambda i: (i, 0))],
        out_shape=[jax.ShapeDtypeStruct((m, d), BF16), jax.ShapeDtypeStruct((m, LANES), F32)],
        compiler_params=_params("parallel"),
        name="rmsnorm_forget",
    )(x, g.reshape(1, d), w_in_t, b_f)


def _head_epilogue(z, qg_ref, kg_ref, outs, rows):
    q_ref, kf_ref, kb_ref, vf_ref, vb_ref, gb_ref, ci_ref = outs
    d = HEAD_DIM
    zq, zk, zv = z[:, 0:d], z[:, d:2 * d], z[:, 2 * d:3 * d]
    q = (zq * _row_rms_scale(zq)) * qg_ref[...]
    k = (zk * _row_rms_scale(zk)) * kg_ref[...]
    q_ref[rows, :] = (q * Q_SCALE).astype(BF16)
    kf_ref[rows, :] = k
    kb_ref[rows, :] = k.astype(BF16)
    vf_ref[rows, :] = zv
    vb_ref[rows, :] = zv.astype(BF16)
    gb_ref[rows, :] = z[:, 3 * d:4 * d]
    ci_ref[rows, :] = z[:, 4 * d:5 * d] * z[:, 5 * d:6 * d]


def _inproj_kernel(u_ref, us_ref, wq_ref, wk_ref, wv_ref, wb_ref, wc_ref, wh_ref, qg_ref, kg_ref,
                   *refs, tm, rc):
    outs, outs_s, w_scr = refs[0:7], refs[7:14], refs[14]
    d = HEAD_DIM

    @pl.when(pl.program_id(1) == 0)
    def _():
        for s, w_ref in enumerate((wq_ref, wk_ref, wv_ref, wb_ref, wc_ref, wh_ref)):
            w_scr[:, s * d:(s + 1) * d] = w_ref[...].T.astype(BF16)
        _head_epilogue(_dot(us_ref[...], w_scr[...]), qg_ref, kg_ref, outs_s, slice(None))

    for r in range(tm // rc):
        rows = slice(r * rc, (r + 1) * rc)
        _head_epilogue(_dot(u_ref[rows, :], w_scr[...]), qg_ref, kg_ref, outs, rows)


def in_projection(u, us, w_in_t, q_gain, k_gain, tm):
    m, dm = u.shape
    ms = us.shape[0]
    nh = N_HEADS
    fixed = lambda h, i: (0, 0)
    conv0 = 3 * ATT_WIDTH + nh

    def att_rows(s):
        return pl.BlockSpec((HEAD_DIM, dm), lambda h, i: (s * nh + h, 0))

    def conv_rows(s):
        return pl.BlockSpec((pl.Element(HEAD_DIM), pl.Element(dm)),
                            lambda h, i: (pl.multiple_of(conv0 + (s * nh + h) * HEAD_DIM, 8), 0))

    head_out = pl.BlockSpec((tm, HEAD_DIM), lambda h, i: (i, h))
    head_out_s = pl.BlockSpec((ms, HEAD_DIM), lambda h, i: (0, h))
    dts = (BF16, F32, BF16, F32, BF16, F32, F32)
    shapes = [jax.ShapeDtypeStruct((m, ATT_WIDTH), dt) for dt in dts]
    shapes += [jax.ShapeDtypeStruct((ms, ATT_WIDTH), dt) for dt in dts]
    outs = pl.pallas_call(
        functools.partial(_inproj_kernel, tm=tm, rc=256),
        grid=(nh, m // tm),
        in_specs=[pl.BlockSpec((tm, dm), lambda h, i: (i, 0)),
                  pl.BlockSpec((ms, dm), fixed),
                  att_rows(0), att_rows(1), att_rows(2), conv_rows(0), conv_rows(1), conv_rows(2),
                  pl.BlockSpec((1, HEAD_DIM), fixed),
                  pl.BlockSpec((1, HEAD_DIM), fixed)],
        out_specs=[head_out] * 7 + [head_out_s] * 7,
        out_shape=shapes,
        scratch_shapes=[pltpu.VMEM((dm, HEAD_TILE), BF16)],
        compiler_params=_params("parallel", "arbitrary"),
        name="in_projection",
    )(u, us, *([w_in_t] * 6), q_gain.reshape(1, HEAD_DIM), k_gain.reshape(1, HEAD_DIM))
    return outs[0:7], outs[7:14]


def _decay_features_kernel(lf_ref, qx_ref, kx_ref, *, t):
    r = lax.broadcasted_iota(jnp.int32, (LANES, LANES), 0)
    c = lax.broadcasted_iota(jnp.int32, (LANES, LANES), 1)
    tri = (c <= r).astype(F32)
    lane = c
    carry = jnp.zeros((1, LANES), F32)
    for j in range(t // LANES):
        rows = slice(j * LANES, (j + 1) * LANES)
        cs = jnp.dot(tri, lf_ref[rows, :], precision=HIGHEST, preferred_element_type=F32) + carry
        carry = cs[LANES - 1:LANES, :]
        pieces = _split3(cs * LOG2E)
        for h in range(N_HEADS):
            cols = slice(h * HEAD_DIM, (h + 1) * HEAD_DIM)
            qx = jnp.where((lane >= N_SPLIT) & (lane < 2 * N_SPLIT), 1.0, 0.0)
            kx = jnp.where(lane < N_SPLIT, 1.0, 0.0)
            for s, piece in enumerate(pieces):
                col = piece[:, h:h + 1]
                qx = jnp.where(lane == s, col, qx)
                kx = jnp.where(lane == N_SPLIT + s, -col, kx)
            qx_ref[rows, cols] = qx.astype(BF16)
            kx_ref[rows, cols] = kx.astype(BF16)


def decay_features(lf, batch, t):
    m = lf.shape[0]
    out = pl.BlockSpec((t, ATT_WIDTH), lambda b: (b, 0))
    return pl.pallas_call(
        functools.partial(_decay_features_kernel, t=t),
        grid=(batch,),
        in_specs=[pl.BlockSpec((t, LANES), lambda b: (b, 0))],
        out_specs=[out, out],
        out_shape=[jax.ShapeDtypeStruct((m, ATT_WIDTH), BF16)] * 2,
        compiler_params=_params("parallel"),
        name="decay_features",
    )(lf)


def _flash_kernel(q_ref, qx_ref, k_ref, kx_ref, v_ref, o_ref, m_scr, l_scr, acc_scr, *, tq, tk):
    qi = pl.program_id(1)
    kj = pl.program_id(2)
    d = HEAD_DIM

    @pl.when(kj == 0)
    def _():
        m_scr[...] = jnp.full(m_scr.shape, NEG_INF, F32)
        l_scr[...] = jnp.zeros(l_scr.shape, F32)
        acc_scr[...] = jnp.zeros(acc_scr.shape, F32)

    def step(masked):
        if masked:
            row = lax.broadcasted_iota(jnp.int32, (tq, tk), 0)
            col = lax.broadcasted_iota(jnp.int32, (tq, tk), 1)
            keep = col <= row
        for h in range(N_HEADS):
            cs = slice(h * d, (h + 1) * d)
            qa = jnp.concatenate([q_ref[:, cs], qx_ref[:, cs]], axis=1)
            ka = jnp.concatenate([k_ref[:, cs], kx_ref[:, cs]], axis=1)
            s = lax.dot_general(qa, ka, (((1,), (1,)), ((), ())), preferred_element_type=F32)
            if masked:
                s = jnp.where(keep, s, NEG_INF)
            m_old = m_scr[h]
            m_new = jnp.maximum(m_old, jnp.max(s, axis=-1, keepdims=True))
            p = jnp.exp2(s - pltpu.repeat(m_new, tk // LANES, axis=1))
            alpha = jnp.exp2(m_old - m_new)
            l_scr[h] = alpha * l_scr[h] + jnp.sum(p, axis=-1, keepdims=True)
            m_scr[h] = m_new
            acc_scr[:, cs] = alpha * acc_scr[:, cs] + _dot(p.astype(BF16), v_ref[:, cs])

    @pl.when(kj < qi)
    def _():
        step(False)

    @pl.when(kj == qi)
    def _():
        step(True)
        for h in range(N_HEADS):
            cs = slice(h * d, (h + 1) * d)
            o_ref[:, cs] = acc_scr[:, cs] / l_scr[h]


def prompt_attention(q, qx, k, kx, v, batch, t, tq):
    m = q.shape[0]
    nq = t // tq
    qrow = lambda b, i, j: (b * nq + i, 0)
    krow = lambda b, i, j: (b * nq + jnp.minimum(i, j), 0)
    qspec = pl.BlockSpec((tq, ATT_WIDTH), qrow)
    kspec = pl.BlockSpec((tq, ATT_WIDTH), krow)
    return pl.pallas_call(
        functools.partial(_flash_kernel, tq=tq, tk=tq),
        grid=(batch, nq, nq),
        in_specs=[qspec, qspec, kspec, kspec, kspec],
        out_specs=qspec,
        out_shape=jax.ShapeDtypeStruct((m, ATT_WIDTH), F32),
        scratch_shapes=[pltpu.VMEM((N_HEADS, tq, LANES), F32),
                        pltpu.VMEM((N_HEADS, tq, LANES), F32),
                        pltpu.VMEM((tq, ATT_WIDTH), F32)],
        compiler_params=_params("parallel", "parallel", "arbitrary"),
        name="prompt_attention",
    )(q, qx, k, kx, v)


PAGES_PER_STEP = 4


def _decode_kernel(pt_ref, q_ref, kn_ref, vn_ref, lfn_ref, *refs, n_steps):
    del pt_ref
    npg = PAGES_PER_STEP
    k_refs, v_refs, lp_refs = refs[0:npg], refs[npg:2 * npg], refs[2 * npg:3 * npg]
    o_ref, piece_scr, m_scr, l_scr, tot_scr, acc_scr = refs[3 * npg:]
    s_id = pl.program_id(1)
    ones = jnp.ones((2 * HEAD_DIM, LANES), BF16)
    qs = q_ref[...].astype(F32)

    @pl.when(s_id == 0)
    def _():
        prod = (kn_ref[...].astype(F32) * qs).astype(BF16)
        m_scr[...] = _dot(prod, ones[0:HEAD_DIM, :])
        l_scr[...] = jnp.ones(l_scr.shape, F32)
        acc_scr[...] = vn_ref[...].astype(F32)
        tot_scr[...] = lfn_ref[...]
        piece_scr[...] = jnp.zeros(piece_scr.shape, F32)

    r = lax.broadcasted_iota(jnp.int32, (PAGE_SIZE, PAGE_SIZE), 0)
    c = lax.broadcasted_iota(jnp.int32, (PAGE_SIZE, PAGE_SIZE), 1)
    later_t = (r > c).astype(F32)
    hh = lax.broadcasted_iota(jnp.int32, (N_HEADS, LANES), 0)
    ll = lax.broadcasted_iota(jnp.int32, (N_HEADS, LANES), 1)
    own = ((ll % N_HEADS) == hh) & (ll < N_SPLIT * N_HEADS)

    tot = tot_scr[...]
    m = m_scr[...]
    l = l_scr[...]
    acc = acc_scr[...]
    for j in range(npg):
        lp_t = lp_refs[j][...]
        dec_t = jnp.dot(lp_t, later_t, precision=HIGHEST, preferred_element_type=F32) + tot
        tot = tot + jnp.sum(lp_t, axis=1, keepdims=True)
        for s, piece in enumerate(_split3(dec_t * LOG2E)):
            piece_scr[s * N_HEADS:(s + 1) * N_HEADS, :] = piece
        pieces = piece_scr[...].T
        spread = jnp.broadcast_to(pieces[:, None, :], (PAGE_SIZE, N_HEADS, LANES))
        extra = jnp.where(own[None], spread, 0.0).astype(BF16).reshape(PAGE_SIZE * N_HEADS, LANES)
        prod = (k_refs[j][...] * qs[None]).astype(BF16).reshape(PAGE_SIZE * N_HEADS, HEAD_DIM)
        s2 = _dot(jnp.concatenate([prod, extra], axis=1), ones).reshape(PAGE_SIZE, N_HEADS, LANES)
        m_new = jnp.maximum(m, jnp.max(s2, axis=0))
        alpha = jnp.exp2(m - m_new)
        p = jnp.exp2(s2 - m_new[None])
        l = alpha * l + jnp.sum(p, axis=0)
        acc = alpha * acc + jnp.sum(p * v_refs[j][...], axis=0)
        m = m_new
    tot_scr[...] = tot
    m_scr[...] = m
    l_scr[...] = l
    acc_scr[...] = acc

    @pl.when(s_id == n_steps - 1)
    def _():
        o_ref[...] = acc / l


def sample_attention(q, k_new, v_new, lf_new, cache_k, cache_v, cache_lf_t, page_table):
    db, n_pages = page_table.shape
    npg = PAGES_PER_STEP
    n_steps = n_pages // npg
    heads = pl.BlockSpec((None, N_HEADS, HEAD_DIM), lambda b, s, pt: (b, 0, 0))

    def page(j):
        return lambda b, s, pt: (pt[b, n_pages - 1 - (npg * s + j)], 0, 0, 0)

    def lf_page(j):
        return lambda b, s, pt: (pt[b, n_pages - 1 - (npg * s + j)], 0, 0)

    kv_specs = [pl.BlockSpec((None, PAGE_SIZE, N_HEADS, HEAD_DIM), page(j)) for j in range(npg)]
    lp_specs = [pl.BlockSpec((None, N_HEADS, PAGE_SIZE), lf_page(j)) for j in range(npg)]
    stat = pltpu.VMEM((N_HEADS, LANES), F32)
    grid_spec = pltpu.PrefetchScalarGridSpec(
        num_scalar_prefetch=1,
        grid=(db, n_steps),
        in_specs=[heads, heads, heads, heads] + kv_specs + kv_specs + lp_specs,
        out_specs=heads,
        scratch_shapes=[pltpu.VMEM((PAGE_SIZE, LANES), F32), stat, stat, stat, stat],
    )
    out = pl.pallas_call(
        functools.partial(_decode_kernel, n_steps=n_steps),
        grid_spec=grid_spec,
        out_shape=jax.ShapeDtypeStruct((db, N_HEADS, HEAD_DIM), F32),
        compiler_params=_params("parallel", "arbitrary"),
        name="sample_attention",
    )(page_table, q.reshape(db, N_HEADS, HEAD_DIM), k_new.reshape(db, N_HEADS, HEAD_DIM),
      v_new.reshape(db, N_HEADS, HEAD_DIM), lf_new,
      *([cache_k] * npg), *([cache_v] * npg), *([cache_lf_t] * npg))
    return out.reshape(db, ATT_WIDTH)


def _mix_store(ao, conv, gb, gao_ref, gco_ref, o_ref):
    conv_o = gb * conv
    o_ref[:, 0:ATT_WIDTH] = ((ao * _row_rms_scale(ao)) * gao_ref[...]).astype(BF16)
    o_ref[:, ATT_WIDTH:D_MODEL] = ((conv_o * _row_rms_scale(conv_o)) * gco_ref[...]).astype(BF16)


def _mix_prompt_kernel(ao_ref, gb_ref, ci_ref, halo_ref, w_ref, gao_ref, gco_ref, o_ref, *,
                       tm, tiles_per_seq):
    ci = ci_ref[...]
    first = pl.program_id(0) % tiles_per_seq == 0
    halo = jnp.where(first, 0.0, halo_ref[...])
    row = lax.broadcasted_iota(jnp.int32, ci.shape, 0)
    prev1 = jnp.where(row == 0, halo[7:8, :], pltpu.roll(ci, 1, axis=0))
    prev2 = jnp.where(row == 0, halo[6:7, :],
                      jnp.where(row == 1, halo[7:8, :], pltpu.roll(ci, 2, axis=0)))
    conv = w_ref[0:1, :] * prev2 + w_ref[1:2, :] * prev1 + w_ref[2:3, :] * ci
    _mix_store(ao_ref[...], conv, gb_ref[...], gao_ref, gco_ref, o_ref)


def mix_prompt(ao, gb, ci, conv_w, g_ao, g_co, t, tm):
    m = ao.shape[0]
    rows = lambda i: (i, 0)
    fixed = lambda i: (0, 0)
    tile = pl.BlockSpec((tm, ATT_WIDTH), rows)
    halo = pl.BlockSpec((8, CONV_WIDTH), lambda i: (jnp.maximum(i * (tm // 8) - 1, 0), 0))
    return pl.pallas_call(
        functools.partial(_mix_prompt_kernel, tm=tm, tiles_per_seq=t // tm),
        grid=(m // tm,),
        in_specs=[tile, tile, tile, halo,
                  pl.BlockSpec((CONV_K, CONV_WIDTH), fixed),
                  pl.BlockSpec((1, ATT_WIDTH), fixed),
                  pl.BlockSpec((1, CONV_WIDTH), fixed)],
        out_specs=pl.BlockSpec((tm, D_MODEL), rows),
        out_shape=jax.ShapeDtypeStruct((m, D_MODEL), BF16),
        compiler_params=_params("parallel"),
        name="mix_prompt",
    )(ao, gb, ci, ci, conv_w, g_ao.reshape(1, -1), g_co.reshape(1, -1))


def _mix_sample_kernel(ao_ref, gb_ref, ci_ref, s0_ref, s1_ref, w_ref, gao_ref, gco_ref, o_ref):
    conv = w_ref[0:1, :] * s0_ref[...] + w_ref[1:2, :] * s1_ref[...] + w_ref[2:3, :] * ci_ref[...]
    _mix_store(ao_ref[...], conv, gb_ref[...], gao_ref, gco_ref, o_ref)


def mix_sample(ao, gb, ci, s0, s1, conv_w, g_ao, g_co):
    m = ao.shape[0]
    full = lambda shape: pl.BlockSpec(shape, lambda i: (0, 0))
    tile = full((m, ATT_WIDTH))
    return pl.pallas_call(
        _mix_sample_kernel,
        grid=(1,),
        in_specs=[tile] * 5 + [full((CONV_K, CONV_WIDTH)), full((1, ATT_WIDTH)), full((1, CONV_WIDTH))],
        out_specs=full((m, D_MODEL)),
        out_shape=jax.ShapeDtypeStruct((m, D_MODEL), BF16),
        compiler_params=_params("arbitrary"),
        name="mix_sample",
    )(ao, gb, ci, s0, s1, conv_w, g_ao.reshape(1, -1), g_co.reshape(1, -1))


def _sample_col(nj):
    return lambda i, j, *_: (0, jnp.where(i == 0, j, nj - 1))


def _first_row_tile():
    return pl.program_id(0) == 0


def _mm_resid_kernel(a_ref, as_ref, w_ref, r_ref, rs_ref, o_ref, os_ref):
    o_ref[...] = r_ref[...] + _dot(a_ref[...], w_ref[...].astype(BF16))

    @pl.when(_first_row_tile())
    def _():
        os_ref[...] = rs_ref[...] + _dot(as_ref[...], w_ref[...].astype(BF16))


def matmul_residual(a, a_s, w, resid, resid_s, tm, tn):
    m, k = a.shape
    ms = a_s.shape[0]
    n = w.shape[1]
    scol = _sample_col(n // tn)
    return pl.pallas_call(
        _mm_resid_kernel,
        grid=(m // tm, n // tn),
        in_specs=[pl.BlockSpec((tm, k), lambda i, j: (i, 0)),
                  pl.BlockSpec((ms, k), lambda i, j: (0, 0)),
                  pl.BlockSpec((k, tn), lambda i, j: (0, j)),
                  pl.BlockSpec((tm, tn), lambda i, j: (i, j)),
                  pl.BlockSpec((ms, tn), scol)],
        out_specs=[pl.BlockSpec((tm, tn), lambda i, j: (i, j)),
                   pl.BlockSpec((ms, tn), scol)],
        out_shape=[jax.ShapeDtypeStruct((m, n), F32), jax.ShapeDtypeStruct((ms, n), F32)],
        compiler_params=_params("arbitrary", "arbitrary"),
        name="matmul_residual",
    )(a, a_s, w, resid, resid_s)


def _relu2(z):
    z = jnp.maximum(z, 0.0)
    return (z * z).astype(BF16)


def _mm_relu2_kernel(a_ref, as_ref, w_ref, o_ref, os_ref):
    o_ref[...] = _relu2(_dot(a_ref[...], w_ref[...].astype(BF16)))

    @pl.when(_first_row_tile())
    def _():
        os_ref[...] = _relu2(_dot(as_ref[...], w_ref[...].astype(BF16)))


def matmul_relu2(a, a_s, w, tm, tn):
    m, k = a.shape
    ms = a_s.shape[0]
    n = w.shape[1]
    scol = _sample_col(n // tn)
    return pl.pallas_call(
        _mm_relu2_kernel,
        grid=(m // tm, n // tn),
        in_specs=[pl.BlockSpec((tm, k), lambda i, j: (i, 0)),
                  pl.BlockSpec((ms, k), lambda i, j: (0, 0)),
                  pl.BlockSpec((k, tn), lambda i, j: (0, j))],
        out_specs=[pl.BlockSpec((tm, tn), lambda i, j: (i, j)),
                   pl.BlockSpec((ms, tn), scol)],
        out_shape=[jax.ShapeDtypeStruct((m, n), BF16), jax.ShapeDtypeStruct((ms, n), BF16)],
        compiler_params=_params("arbitrary", "arbitrary"),
        name="matmul_relu2",
    )(a, a_s, w)


def _mm_down_kernel(a_ref, as_ref, w_ref, r_ref, rs_ref, o_ref, os_ref, acc_ref, accs_ref, *, nk):
    kk = pl.program_id(2)

    @pl.when(kk == 0)
    def _():
        acc_ref[...] = r_ref[...]

    acc_ref[...] += _dot(a_ref[...], w_ref[...].astype(BF16))

    @pl.when(kk == nk - 1)
    def _():
        o_ref[...] = acc_ref[...]

    @pl.when(_first_row_tile())
    def _():
        @pl.when(kk == 0)
        def _():
            accs_ref[...] = rs_ref[...]

        accs_ref[...] += _dot(as_ref[...], w_ref[...].astype(BF16))

        @pl.when(kk == nk - 1)
        def _():
            os_ref[...] = accs_ref[...]


def matmul_down_residual(a, a_s, w, resid, resid_s, tm, tn, tk):
    m, k = a.shape
    ms = a_s.shape[0]
    n = w.shape[1]
    nk = k // tk
    scol = _sample_col(n // tn)
    return pl.pallas_call(
        functools.partial(_mm_down_kernel, nk=nk),
        grid=(m // tm, n // tn, nk),
        in_specs=[pl.BlockSpec((tm, tk), lambda i, j, kk: (i, kk)),
                  pl.BlockSpec((ms, tk), lambda i, j, kk: (0, jnp.where(i == 0, kk, nk - 1))),
                  pl.BlockSpec((tk, tn), lambda i, j, kk: (kk, j)),
                  pl.BlockSpec((tm, tn), lambda i, j, kk: (i, j)),
                  pl.BlockSpec((ms, tn), scol)],
        out_specs=[pl.BlockSpec((tm, tn), lambda i, j, kk: (i, j)),
                   pl.BlockSpec((ms, tn), scol)],
        out_shape=[jax.ShapeDtypeStruct((m, n), F32), jax.ShapeDtypeStruct((ms, n), F32)],
        scratch_shapes=[pltpu.VMEM((tm, tn), F32), pltpu.VMEM((ms, tn), F32)],
        compiler_params=_params("arbitrary", "arbitrary", "arbitrary"),
        name="matmul_down_residual",
    )(a, a_s, w, resid, resid_s)


def _gated(u, wg, p, we, h):
    gate = jax.nn.sigmoid(_dot(u, wg.astype(BF16)))
    return h + gate * _dot(p.astype(BF16), we.astype(BF16))


def _ple_kernel(u_ref, us_ref, wg_ref, p_ref, ps_ref, we_ref, h_ref, hs_ref, o_ref, os_ref):
    o_ref[...] = _gated(u_ref[...], wg_ref[...], p_ref[...], we_ref[...], h_ref[...])

    @pl.when(_first_row_tile())
    def _():
        os_ref[...] = _gated(us_ref[...], wg_ref[...], ps_ref[...], we_ref[...], hs_ref[...])


def gated_embedding(u, u_s, w_pg, p, p_s, w_pe, h, h_s, tm, tn):
    m, k = u.shape
    ms = u_s.shape[0]
    n = w_pg.shape[1]
    kp = p.shape[1]
    scol = _sample_col(n // tn)
    return pl.pallas_call(
        _ple_kernel,
        grid=(m // tm, n // tn),
        in_specs=[pl.BlockSpec((tm, k), lambda i, j: (i, 0)),
                  pl.BlockSpec((ms, k), lambda i, j: (0, 0)),
                  pl.BlockSpec((k, tn), lambda i, j: (0, j)),
                  pl.BlockSpec((tm, kp), lambda i, j: (i, 0)),
                  pl.BlockSpec((ms, kp), lambda i, j: (0, 0)),
                  pl.BlockSpec((kp, tn), lambda i, j: (0, j)),
                  pl.BlockSpec((tm, tn), lambda i, j: (i, j)),
                  pl.BlockSpec((ms, tn), scol)],
        out_specs=[pl.BlockSpec((tm, tn), lambda i, j: (i, j)),
                   pl.BlockSpec((ms, tn), scol)],
        out_shape=[jax.ShapeDtypeStruct((m, n), F32), jax.ShapeDtypeStruct((ms, n), F32)],
        compiler_params=_params("arbitrary", "arbitrary"),
        name="gated_embedding",
    )(u, u_s, w_pg, p, p_s, w_pe, h, h_s)


def kernel(x_prompt, x_sample, cache_k, cache_v, cache_logf, state_conv, page_table,
           p_prompt, p_sample, g_mix, w_in, b_f, q_gain, k_gain, conv_w,
           g_attn_out, g_conv_out, w_out, g_mlp, w_up, w_down, g_ple, w_pg, w_pe):
    assert w_in.shape[0] == 1, "single-layer trunk"
    batch, seq, _ = x_prompt.shape
    db = x_sample.shape[0]
    tm = 1024
    tr = 512

    w_in_t = jnp.transpose(w_in[0])
    b_fp = jnp.pad(b_f[0], (0, LANES - N_HEADS)).reshape(1, LANES)

    xp = x_prompt.reshape(batch * seq, D_MODEL)
    xs = x_sample.reshape(db, D_MODEL)
    u, lf = rmsnorm_forget(xp, g_mix[0], w_in_t, b_fp, tr)
    us, lfs = rmsnorm_forget(xs, g_mix[0], w_in_t, b_fp, db)
    (q, kf, kb, vf, vb, gb, ci), (qs, kfs, kbs, vfs, vbs, gbs, cis) = in_projection(
        u, us, w_in_t, q_gain[0], k_gain[0], tr)

    lfs_rep = jnp.broadcast_to(lfs[:, :N_HEADS, None], (db, N_HEADS, LANES))
    cache_lf_t = jnp.transpose(cache_logf[0], (0, 2, 1))
    aos = sample_attention(qs, kbs, vbs, lfs_rep, cache_k[0], cache_v[0], cache_lf_t, page_table)
    st = state_conv[0]
    os_ = mix_sample(aos, gbs, cis, st[:, 0, :], st[:, 1, :], conv_w[0], g_attn_out[0], g_conv_out[0])

    qx, kx = decay_features(lf, batch, seq)
    ao = prompt_attention(q, qx, kb, kx, vb, batch, seq, 256)
    o = mix_prompt(ao, gb, ci, conv_w[0], g_attn_out[0], g_conv_out[0], seq, 256)

    h, hs = matmul_residual(o, os_, w_out[0], xp, xs, tm, 512)
    a, a_s = matmul_relu2(rmsnorm_bf16(h, g_mlp[0], tr), rmsnorm_bf16(hs, g_mlp[0], db),
                          w_up[0], tm, 512)
    h, hs = matmul_down_residual(a, a_s, w_down[0], h, hs, tm, 1024, 2048)
    yp, ys = gated_embedding(rmsnorm_bf16(h, g_ple[0], tr), rmsnorm_bf16(hs, g_ple[0], db),
                             w_pg[0], p_prompt[0].reshape(batch * seq, PLE_DIM),
                             p_sample[0].reshape(db, PLE_DIM), w_pe[0], h, hs, tm, 512)

    k_prompt = kf.reshape(1, batch, seq, N_HEADS, HEAD_DIM)
    v_prompt = vf.reshape(1, batch, seq, N_HEADS, HEAD_DIM)
    logf_prompt = lf[:, :N_HEADS].reshape(1, batch, seq, N_HEADS)
    conv_prompt = ci.reshape(batch, seq, CONV_WIDTH)[None, :, seq - (CONV_K - 1):, :]
    k_sample = kfs.reshape(1, db, 1, N_HEADS, HEAD_DIM)
    v_sample = vfs.reshape(1, db, 1, N_HEADS, HEAD_DIM)
    logf_sample = lfs[:, :N_HEADS].reshape(1, db, 1, N_HEADS)
    conv_sample = jnp.stack([st[:, 1, :], cis], axis=1)[None]

    return (yp.reshape(batch, seq, D_MODEL), ys.reshape(db, 1, D_MODEL),
            k_prompt, v_prompt, logf_prompt, conv_prompt,
            k_sample, v_sample, logf_sample, conv_sample)
```

```python
import functools
import math

import jax
import jax.numpy as jnp
from jax import lax
from jax.experimental import pallas as pl
from jax.experimental.pallas import tpu as pltpu

F32 = jnp.float32
BF16 = jnp.bfloat16

D_MODEL = 4096
HEAD_DIM = 128
N_HEADS = 16
ATT_WIDTH = N_HEADS * HEAD_DIM
CONV_WIDTH = D_MODEL - ATT_WIDTH
CONV_K = 3
PLE_DIM = 256
PAGE_SIZE = 128
EPS = 1e-6
NEG_INF = -1e30
LOG2E = math.log2(math.e)
Q_SCALE = HEAD_DIM ** -0.5 * LOG2E
LANES = 128
HEAD_TILE = 6 * HEAD_DIM
VMEM_LIMIT = 60 * 1024 * 1024
HIGHEST = lax.Precision.HIGHEST
N_SPLIT = 3


def _params(*sem):
    return pltpu.CompilerParams(dimension_semantics=sem, vmem_limit_bytes=VMEM_LIMIT)


def _log_sigmoid(x):
    return jnp.minimum(x, 0.0) - jnp.log1p(jnp.exp(-jnp.abs(x)))


def _row_rms_scale(x):
    return lax.rsqrt(jnp.mean(x * x, axis=-1, keepdims=True) + EPS)


def _split3(x):
    hi = x.astype(BF16).astype(F32)
    r = x - hi
    mid = r.astype(BF16).astype(F32)
    lo = (r - mid).astype(BF16).astype(F32)
    return hi, mid, lo


def _dot(a, b):
    return jnp.dot(a, b, preferred_element_type=F32)


def _dot_nt(a, b):
    return lax.dot_general(a, b, (((1,), (1,)), ((), ())), preferred_element_type=F32)


def _rmsnorm_kernel(x_ref, g_ref, o_ref):
    x = x_ref[...]
    o_ref[...] = ((x * _row_rms_scale(x)) * g_ref[...]).astype(o_ref.dtype)


def rmsnorm_bf16(x, g, tm):
    m, d = x.shape
    return pl.pallas_call(
        _rmsnorm_kernel,
        grid=(m // tm,),
        in_specs=[pl.BlockSpec((tm, d), lambda i: (i, 0)),
                  pl.BlockSpec((1, d), lambda i: (0, 0))],
        out_specs=pl.BlockSpec((tm, d), lambda i: (i, 0)),
        out_shape=jax.ShapeDtypeStruct((m, d), BF16),
        compiler_params=_params("parallel"),
        name="rmsnorm_bf16",
    )(x, g.reshape(1, d))


def _rmsnorm_forget_kernel(x_ref, g_ref, wf_ref, bf_ref, u_ref, lf_ref):
    x = x_ref[...]
    u = ((x * _row_rms_scale(x)) * g_ref[...]).astype(BF16)
    u_ref[...] = u
    lf_ref[...] = _log_sigmoid(_dot_nt(u, wf_ref[...].astype(BF16)) + bf_ref[...])


def rmsnorm_forget(x, g, w_in_t, b_f, tm):
    m, d = x.shape
    fixed = lambda i: (0, 0)
    f_block = 3 * ATT_WIDTH // LANES
    return pl.pallas_call(
        _rmsnorm_forget_kernel,
        grid=(m // tm,),
        in_specs=[pl.BlockSpec((tm, d), lambda i: (i, 0)),
                  pl.BlockSpec((1, d), fixed),
                  pl.BlockSpec((LANES, d), lambda i: (f_block, 0)),
                  pl.BlockSpec((1, LANES), fixed)],
        out_specs=[pl.BlockSpec((tm, d), lambda i: (i, 0)),
                   pl.BlockSpec((tm, LANES), lambda i: (i, 0))],
        out_shape=[jax.ShapeDtypeStruct((m, d), BF16), jax.ShapeDtypeStruct((m, LANES), F32)],
        compiler_params=_params("parallel"),
        name="rmsnorm_forget",
    )(x, g.reshape(1, d), w_in_t, b_f)


def _head_epilogue(z, qg_ref, kg_ref, outs, rows):
    q_ref, kf_ref, kb_ref, vf_ref, vb_ref, gb_ref, ci_ref = outs
    d = HEAD_DIM
    zq, zk, zv = z[:, 0:d], z[:, d:2 * d], z[:, 2 * d:3 * d]
    q = (zq * _row_rms_scale(zq)) * qg_ref[...]
    k = (zk * _row_rms_scale(zk)) * kg_ref[...]
    q_ref[rows, :] = (q * Q_SCALE).astype(BF16)
    kf_ref[rows, :] = k
    kb_ref[rows, :] = k.astype(BF16)
    vf_ref[rows, :] = zv
    vb_ref[rows, :] = zv.astype(BF16)
    gb_ref[rows, :] = z[:, 3 * d:4 * d]
    ci_ref[rows, :] = z[:, 4 * d:5 * d] * z[:, 5 * d:6 * d]


def _inproj_kernel(u_ref, us_ref, wq_ref, wk_ref, wv_ref, wb_ref, wc_ref, wh_ref, qg_ref, kg_ref,
                   *refs, tm, rc):
    outs, outs_s, w_scr = refs[0:7], refs[7:14], refs[14]
    d = HEAD_DIM

    @pl.when(pl.program_id(1) == 0)
    def _():
        for s, w_ref in enumerate((wq_ref, wk_ref, wv_ref, wb_ref, wc_ref, wh_ref)):
            w_scr[:, s * d:(s + 1) * d] = w_ref[...].T.astype(BF16)
        _head_epilogue(_dot(us_ref[...], w_scr[...]), qg_ref, kg_ref, outs_s, slice(None))

    for r in range(tm // rc):
        rows = slice(r * rc, (r + 1) * rc)
        _head_epilogue(_dot(u_ref[rows, :], w_scr[...]), qg_ref, kg_ref, outs, rows)


def in_projection(u, us, w_in_t, q_gain, k_gain, tm):
    m, dm = u.shape
    ms = us.shape[0]
    nh = N_HEADS
    fixed = lambda h, i: (0, 0)
    conv0 = 3 * ATT_WIDTH + nh

    def att_rows(s):
        return pl.BlockSpec((HEAD_DIM, dm), lambda h, i: (s * nh + h, 0))

    def conv_rows(s):
        return pl.BlockSpec((pl.Element(HEAD_DIM), pl.Element(dm)),
                            lambda h, i: (pl.multiple_of(conv0 + (s * nh + h) * HEAD_DIM, 8), 0))

    head_out = pl.BlockSpec((tm, HEAD_DIM), lambda h, i: (i, h))
    head_out_s = pl.BlockSpec((ms, HEAD_DIM), lambda h, i: (0, h))
    dts = (BF16, F32, BF16, F32, BF16, F32, F32)
    shapes = [jax.ShapeDtypeStruct((m, ATT_WIDTH), dt) for dt in dts]
    shapes += [jax.ShapeDtypeStruct((ms, ATT_WIDTH), dt) for dt in dts]
    outs = pl.pallas_call(
        functools.partial(_inproj_kernel, tm=tm, rc=256),
        grid=(nh, m // tm),
        in_specs=[pl.BlockSpec((tm, dm), lambda h, i: (i, 0)),
                  pl.BlockSpec((ms, dm), fixed),
                  att_rows(0), att_rows(1), att_rows(2), conv_rows(0), conv_rows(1), conv_rows(2),
                  pl.BlockSpec((1, HEAD_DIM), fixed),
                  pl.BlockSpec((1, HEAD_DIM), fixed)],
        out_specs=[head_out] * 7 + [head_out_s] * 7,
        out_shape=shapes,
        scratch_shapes=[pltpu.VMEM((dm, HEAD_TILE), BF16)],
        compiler_params=_params("parallel", "arbitrary"),
        name="in_projection",
    )(u, us, *([w_in_t] * 6), q_gain.reshape(1, HEAD_DIM), k_gain.reshape(1, HEAD_DIM))
    return outs[0:7], outs[7:14]


def _decay_features_kernel(lf_ref, qx_ref, kx_ref, *, t):
    r = lax.broadcasted_iota(jnp.int32, (LANES, LANES), 0)
    c = lax.broadcasted_iota(jnp.int32, (LANES, LANES), 1)
    tri = (c <= r).astype(F32)
    lane = c
    carry = jnp.zeros((1, LANES), F32)
    for j in range(t // LANES):
        rows = slice(j * LANES, (j + 1) * LANES)
        cs = jnp.dot(tri, lf_ref[rows, :], precision=HIGHEST, preferred_element_type=F32) + carry
        carry = cs[LANES - 1:LANES, :]
        pieces = _split3(cs * LOG2E)
        for h in range(N_HEADS):
            cols = slice(h * HEAD_DIM, (h + 1) * HEAD_DIM)
            qx = jnp.where((lane >= N_SPLIT) & (lane < 2 * N_SPLIT), 1.0, 0.0)
            kx = jnp.where(lane < N_SPLIT, 1.0, 0.0)
            for s, piece in enumerate(pieces):
                col = piece[:, h:h + 1]
                qx = jnp.where(lane == s, col, qx)
                kx = jnp.where(lane == N_SPLIT + s, -col, kx)
            qx_ref[rows, cols] = qx.astype(BF16)
            kx_ref[rows, cols] = kx.astype(BF16)


def decay_features(lf, batch, t):
    m = lf.shape[0]
    out = pl.BlockSpec((t, ATT_WIDTH), lambda b: (b, 0))
    return pl.pallas_call(
        functools.partial(_decay_features_kernel, t=t),
        grid=(batch,),
        in_specs=[pl.BlockSpec((t, LANES), lambda b: (b, 0))],
        out_specs=[out, out],
        out_shape=[jax.ShapeDtypeStruct((m, ATT_WIDTH), BF16)] * 2,
        compiler_params=_params("parallel"),
        name="decay_features",
    )(lf)


def _flash_kernel(q_ref, qx_ref, k_ref, kx_ref, v_ref, o_ref, m_scr, l_scr, acc_scr, *, tq, tk):
    qi = pl.program_id(1)
    kj = pl.program_id(2)
    d = HEAD_DIM
    last = (qi * tq) // tk

    @pl.when(kj == 0)
    def _():
        m_scr[...] = jnp.full(m_scr.shape, NEG_INF, F32)
        l_scr[...] = jnp.zeros(l_scr.shape, F32)
        acc_scr[...] = jnp.zeros(acc_scr.shape, F32)

    def step(masked):
        if masked:
            row = lax.broadcasted_iota(jnp.int32, (tq, tk), 0) + (qi * tq - kj * tk)
            col = lax.broadcasted_iota(jnp.int32, (tq, tk), 1)
            keep = col <= row
        for h in range(N_HEADS):
            cs = slice(h * d, (h + 1) * d)
            qa = jnp.concatenate([q_ref[:, cs], qx_ref[:, cs]], axis=1)
            ka = jnp.concatenate([k_ref[:, cs], kx_ref[:, cs]], axis=1)
            s = lax.dot_general(qa, ka, (((1,), (1,)), ((), ())), preferred_element_type=F32)
            if masked:
                s = jnp.where(keep, s, NEG_INF)
            m_old = m_scr[h]
            m_new = jnp.maximum(m_old, jnp.max(s, axis=-1, keepdims=True))
            p = jnp.exp2(s - pltpu.repeat(m_new, tk // LANES, axis=1))
            alpha = jnp.exp2(m_old - m_new)
            l_scr[h] = alpha * l_scr[h] + jnp.sum(p, axis=-1, keepdims=True)
            m_scr[h] = m_new
            acc_scr[:, cs] = alpha * acc_scr[:, cs] + _dot(p.astype(BF16), v_ref[:, cs])

    @pl.when(kj < last)
    def _():
        step(False)

    @pl.when(kj == last)
    def _():
        step(True)
        for h in range(N_HEADS):
            cs = slice(h * d, (h + 1) * d)
            o_ref[:, cs] = acc_scr[:, cs] / l_scr[h]


def prompt_attention(q, qx, k, kx, v, batch, t, tq, tk):
    m = q.shape[0]
    nq, nk = t // tq, t // tk
    qrow = lambda b, i, j: (b * nq + i, 0)
    krow = lambda b, i, j: (b * nk + jnp.minimum(j, (i * tq) // tk), 0)
    qspec = pl.BlockSpec((tq, ATT_WIDTH), qrow)
    kspec = pl.BlockSpec((tk, ATT_WIDTH), krow)
    return pl.pallas_call(
        functools.partial(_flash_kernel, tq=tq, tk=tk),
        grid=(batch, nq, nk),
        in_specs=[qspec, qspec, kspec, kspec, kspec],
        out_specs=qspec,
        out_shape=jax.ShapeDtypeStruct((m, ATT_WIDTH), F32),
        scratch_shapes=[pltpu.VMEM((N_HEADS, tq, LANES), F32),
                        pltpu.VMEM((N_HEADS, tq, LANES), F32),
                        pltpu.VMEM((tq, ATT_WIDTH), F32)],
        compiler_params=_params("parallel", "parallel", "arbitrary"),
        name="prompt_attention",
    )(q, qx, k, kx, v)


PAGES_PER_STEP = 8


def _decode_kernel(pt_ref, q_ref, kn_ref, vn_ref, lfn_ref, *refs, n_steps):
    del pt_ref
    npg = PAGES_PER_STEP
    k_refs, v_refs, lp_refs = refs[0:npg], refs[npg:2 * npg], refs[2 * npg:3 * npg]
    o_ref, s_scr, piece_scr, m_scr, l_scr, tot_scr, acc_scr = refs[3 * npg:]
    s_id = pl.program_id(1)
    ones = jnp.ones((2 * HEAD_DIM, LANES), BF16)
    qs = q_ref[...].astype(F32)

    @pl.when(s_id == 0)
    def _():
        prod = (kn_ref[...].astype(F32) * qs).astype(BF16)
        m_scr[...] = _dot(prod, ones[0:HEAD_DIM, :])
        l_scr[...] = jnp.ones(l_scr.shape, F32)
        acc_scr[...] = vn_ref[...].astype(F32)
        tot_scr[...] = lfn_ref[...]
        piece_scr[...] = jnp.zeros(piece_scr.shape, F32)

    r = lax.broadcasted_iota(jnp.int32, (PAGE_SIZE, PAGE_SIZE), 0)
    c = lax.broadcasted_iota(jnp.int32, (PAGE_SIZE, PAGE_SIZE), 1)
    later_t = (r > c).astype(F32)
    hh = lax.broadcasted_iota(jnp.int32, (N_HEADS, LANES), 0)
    ll = lax.broadcasted_iota(jnp.int32, (N_HEADS, LANES), 1)
    own = ((ll % N_HEADS) == hh) & (ll < N_SPLIT * N_HEADS)

    tot = tot_scr[...]
    for j in range(npg):
        lp_t = lp_refs[j][...]
        dec_t = jnp.dot(lp_t, later_t, precision=HIGHEST, preferred_element_type=F32) + tot
        tot = tot + jnp.sum(lp_t, axis=1, keepdims=True)
        for s, piece in enumerate(_split3(dec_t * LOG2E)):
            piece_scr[j, s * N_HEADS:(s + 1) * N_HEADS, :] = piece
        pieces = piece_scr[j].T
        spread = jnp.broadcast_to(pieces[:, None, :], (PAGE_SIZE, N_HEADS, LANES))
        extra = jnp.where(own[None], spread, 0.0).astype(BF16).reshape(PAGE_SIZE * N_HEADS, LANES)
        prod = (k_refs[j][...] * qs[None]).astype(BF16).reshape(PAGE_SIZE * N_HEADS, HEAD_DIM)
        s_scr[j] = _dot(jnp.concatenate([prod, extra], axis=1), ones)

    m = m_scr[...]
    l = l_scr[...]
    acc = acc_scr[...]
    for j in range(npg):
        s2 = s_scr[j].reshape(PAGE_SIZE, N_HEADS, LANES)
        m_new = jnp.maximum(m, jnp.max(s2, axis=0))
        alpha = jnp.exp2(m - m_new)
        p = jnp.exp2(s2 - m_new[None])
        l = alpha * l + jnp.sum(p, axis=0)
        acc = alpha * acc + jnp.sum(p * v_refs[j][...], axis=0)
        m = m_new
    tot_scr[...] = tot
    m_scr[...] = m
    l_scr[...] = l
    acc_scr[...] = acc

    @pl.when(s_id == n_steps - 1)
    def _():
        o_ref[...] = acc / l


def sample_attention(q, k_new, v_new, lf_new, cache_k, cache_v, cache_lf_t, page_table):
    db, n_pages = page_table.shape
    npg = PAGES_PER_STEP
    n_steps = n_pages // npg
    heads = pl.BlockSpec((None, N_HEADS, HEAD_DIM), lambda b, s, pt: (b, 0, 0))

    def page(j):
        return lambda b, s, pt: (pt[b, n_pages - 1 - (npg * s + j)], 0, 0, 0)

    def lf_page(j):
        return lambda b, s, pt: (pt[b, n_pages - 1 - (npg * s + j)], 0, 0)

    kv_specs = [pl.BlockSpec((None, PAGE_SIZE, N_HEADS, HEAD_DIM), page(j)) for j in range(npg)]
    lp_specs = [pl.BlockSpec((None, N_HEADS, PAGE_SIZE), lf_page(j)) for j in range(npg)]
    stat = pltpu.VMEM((N_HEADS, LANES), F32)
    grid_spec = pltpu.PrefetchScalarGridSpec(
        num_scalar_prefetch=1,
        grid=(db, n_steps),
        in_specs=[heads, heads, heads, heads] + kv_specs + kv_specs + lp_specs,
        out_specs=heads,
        scratch_shapes=[pltpu.VMEM((npg, PAGE_SIZE * N_HEADS, LANES), F32),
                        pltpu.VMEM((npg, PAGE_SIZE, LANES), F32), stat, stat, stat, stat],
    )
    out = pl.pallas_call(
        functools.partial(_decode_kernel, n_steps=n_steps),
        grid_spec=grid_spec,
        out_shape=jax.ShapeDtypeStruct((db, N_HEADS, HEAD_DIM), F32),
        compiler_params=_params("parallel", "arbitrary"),
        name="sample_attention",
    )(page_table, q.reshape(db, N_HEADS, HEAD_DIM), k_new.reshape(db, N_HEADS, HEAD_DIM),
      v_new.reshape(db, N_HEADS, HEAD_DIM), lf_new,
      *([cache_k] * npg), *([cache_v] * npg), *([cache_lf_t] * npg))
    return out.reshape(db, ATT_WIDTH)


def _mix_store(ao, conv, gb, gao_ref, gco_ref, o_ref):
    conv_o = gb * conv
    o_ref[:, 0:ATT_WIDTH] = ((ao * _row_rms_scale(ao)) * gao_ref[...]).astype(BF16)
    o_ref[:, ATT_WIDTH:D_MODEL] = ((conv_o * _row_rms_scale(conv_o)) * gco_ref[...]).astype(BF16)


def _mix_prompt_kernel(ao_ref, gb_ref, ci_ref, halo_ref, w_ref, gao_ref, gco_ref, o_ref, *,
                       tm, tiles_per_seq):
    ci = ci_ref[...]
    first = pl.program_id(0) % tiles_per_seq == 0
    halo = jnp.where(first, 0.0, halo_ref[...])
    row = lax.broadcasted_iota(jnp.int32, ci.shape, 0)
    prev1 = jnp.where(row == 0, halo[7:8, :], pltpu.roll(ci, 1, axis=0))
    prev2 = jnp.where(row == 0, halo[6:7, :],
                      jnp.where(row == 1, halo[7:8, :], pltpu.roll(ci, 2, axis=0)))
    conv = w_ref[0:1, :] * prev2 + w_ref[1:2, :] * prev1 + w_ref[2:3, :] * ci
    _mix_store(ao_ref[...], conv, gb_ref[...], gao_ref, gco_ref, o_ref)


def mix_prompt(ao, gb, ci, conv_w, g_ao, g_co, t, tm):
    m = ao.shape[0]
    rows = lambda i: (i, 0)
    fixed = lambda i: (0, 0)
    tile = pl.BlockSpec((tm, ATT_WIDTH), rows)
    halo = pl.BlockSpec((8, CONV_WIDTH), lambda i: (jnp.maximum(i * (tm // 8) - 1, 0), 0))
    return pl.pallas_call(
        functools.partial(_mix_prompt_kernel, tm=tm, tiles_per_seq=t // tm),
        grid=(m // tm,),
        in_specs=[tile, tile, tile, halo,
                  pl.BlockSpec((CONV_K, CONV_WIDTH), fixed),
                  pl.BlockSpec((1, ATT_WIDTH), fixed),
                  pl.BlockSpec((1, CONV_WIDTH), fixed)],
        out_specs=pl.BlockSpec((tm, D_MODEL), rows),
        out_shape=jax.ShapeDtypeStruct((m, D_MODEL), BF16),
        compiler_params=_params("parallel"),
        name="mix_prompt",
    )(ao, gb, ci, ci, conv_w, g_ao.reshape(1, -1), g_co.reshape(1, -1))


def _mix_sample_kernel(ao_ref, gb_ref, ci_ref, s0_ref, s1_ref, w_ref, gao_ref, gco_ref, o_ref):
    conv = w_ref[0:1, :] * s0_ref[...] + w_ref[1:2, :] * s1_ref[...] + w_ref[2:3, :] * ci_ref[...]
    _mix_store(ao_ref[...], conv, gb_ref[...], gao_ref, gco_ref, o_ref)


def mix_sample(ao, gb, ci, s0, s1, conv_w, g_ao, g_co):
    m = ao.shape[0]
    full = lambda shape: pl.BlockSpec(shape, lambda i: (0, 0))
    tile = full((m, ATT_WIDTH))
    return pl.pallas_call(
        _mix_sample_kernel,
        grid=(1,),
        in_specs=[tile] * 5 + [full((CONV_K, CONV_WIDTH)), full((1, ATT_WIDTH)), full((1, CONV_WIDTH))],
        out_specs=full((m, D_MODEL)),
        out_shape=jax.ShapeDtypeStruct((m, D_MODEL), BF16),
        compiler_params=_params("arbitrary"),
        name="mix_sample",
    )(ao, gb, ci, s0, s1, conv_w, g_ao.reshape(1, -1), g_co.reshape(1, -1))


def _sample_col(nj):
    return lambda i, j, *_: (0, jnp.where(i == 0, j, nj - 1))


def _first_row_tile():
    return pl.program_id(0) == 0


def _row_resident(shape, index_map):
    return pl.BlockSpec(shape, index_map, pipeline_mode=pl.Buffered(1))


def _mm_resid_kernel(a_ref, as_ref, w_ref, r_ref, rs_ref, o_ref, os_ref):
    o_ref[...] = r_ref[...] + _dot(a_ref[...], w_ref[...].astype(BF16))

    @pl.when(_first_row_tile())
    def _():
        os_ref[...] = rs_ref[...] + _dot(as_ref[...], w_ref[...].astype(BF16))


def matmul_residual(a, a_s, w, resid, resid_s, tm, tn):
    m, k = a.shape
    ms = a_s.shape[0]
    n = w.shape[1]
    scol = _sample_col(n // tn)
    return pl.pallas_call(
        _mm_resid_kernel,
        grid=(m // tm, n // tn),
        in_specs=[_row_resident((tm, k), lambda i, j: (i, 0)),
                  _row_resident((ms, k), lambda i, j: (0, 0)),
                  pl.BlockSpec((k, tn), lambda i, j: (0, j)),
                  pl.BlockSpec((tm, tn), lambda i, j: (i, j)),
                  pl.BlockSpec((ms, tn), scol)],
        out_specs=[pl.BlockSpec((tm, tn), lambda i, j: (i, j)),
                   pl.BlockSpec((ms, tn), scol)],
        out_shape=[jax.ShapeDtypeStruct((m, n), F32), jax.ShapeDtypeStruct((ms, n), F32)],
        compiler_params=_params("arbitrary", "arbitrary"),
        name="matmul_residual",
    )(a, a_s, w, resid, resid_s)


def _relu2(z):
    z = jnp.maximum(z, 0.0)
    return (z * z).astype(BF16)


def _mm_relu2_kernel(a_ref, as_ref, w_ref, o_ref, os_ref):
    o_ref[...] = _relu2(_dot(a_ref[...], w_ref[...].astype(BF16)))

    @pl.when(_first_row_tile())
    def _():
        os_ref[...] = _relu2(_dot(as_ref[...], w_ref[...].astype(BF16)))


def matmul_relu2(a, a_s, w, tm, tn):
    m, k = a.shape
    ms = a_s.shape[0]
    n = w.shape[1]
    scol = _sample_col(n // tn)
    return pl.pallas_call(
        _mm_relu2_kernel,
        grid=(m // tm, n // tn),
        in_specs=[_row_resident((tm, k), lambda i, j: (i, 0)),
                  _row_resident((ms, k), lambda i, j: (0, 0)),
                  pl.BlockSpec((k, tn), lambda i, j: (0, j))],
        out_specs=[pl.BlockSpec((tm, tn), lambda i, j: (i, j)),
                   pl.BlockSpec((ms, tn), scol)],
        out_shape=[jax.ShapeDtypeStruct((m, n), BF16), jax.ShapeDtypeStruct((ms, n), BF16)],
        compiler_params=_params("arbitrary", "arbitrary"),
        name="matmul_relu2",
    )(a, a_s, w)


def _mm_down_kernel(a_ref, as_ref, w_ref, r_ref, rs_ref, o_ref, os_ref, acc_ref, accs_ref, *, nk):
    kk = pl.program_id(2)

    @pl.when(kk == 0)
    def _():
        acc_ref[...] = r_ref[...]

    acc_ref[...] += _dot(a_ref[...], w_ref[...].astype(BF16))

    @pl.when(kk == nk - 1)
    def _():
        o_ref[...] = acc_ref[...]

    @pl.when(_first_row_tile())
    def _():
        @pl.when(kk == 0)
        def _():
            accs_ref[...] = rs_ref[...]

        accs_ref[...] += _dot(as_ref[...], w_ref[...].astype(BF16))

        @pl.when(kk == nk - 1)
        def _():
            os_ref[...] = accs_ref[...]


def matmul_down_residual(a, a_s, w, resid, resid_s, tm, tn, tk):
    m, k = a.shape
    ms = a_s.shape[0]
    n = w.shape[1]
    nk = k // tk
    scol = _sample_col(n // tn)
    return pl.pallas_call(
        functools.partial(_mm_down_kernel, nk=nk),
        grid=(m // tm, n // tn, nk),
        in_specs=[pl.BlockSpec((tm, tk), lambda i, j, kk: (i, kk)),
                  pl.BlockSpec((ms, tk), lambda i, j, kk: (0, jnp.where(i == 0, kk, nk - 1))),
                  pl.BlockSpec((tk, tn), lambda i, j, kk: (kk, j)),
                  pl.BlockSpec((tm, tn), lambda i, j, kk: (i, j)),
                  pl.BlockSpec((ms, tn), scol)],
        out_specs=[pl.BlockSpec((tm, tn), lambda i, j, kk: (i, j)),
                   pl.BlockSpec((ms, tn), scol)],
        out_shape=[jax.ShapeDtypeStruct((m, n), F32), jax.ShapeDtypeStruct((ms, n), F32)],
        scratch_shapes=[pltpu.VMEM((tm, tn), F32), pltpu.VMEM((ms, tn), F32)],
        compiler_params=_params("arbitrary", "arbitrary", "arbitrary"),
        name="matmul_down_residual",
    )(a, a_s, w, resid, resid_s)


def _gated(u, wg, p, we, h):
    gate = jax.nn.sigmoid(_dot(u, wg.astype(BF16)))
    return h + gate * _dot(p.astype(BF16), we.astype(BF16))


def _ple_kernel(u_ref, us_ref, wg_ref, p_ref, ps_ref, we_ref, h_ref, hs_ref, o_ref, os_ref):
    o_ref[...] = _gated(u_ref[...], wg_ref[...], p_ref[...], we_ref[...], h_ref[...])

    @pl.when(_first_row_tile())
    def _():
        os_ref[...] = _gated(us_ref[...], wg_ref[...], ps_ref[...], we_ref[...], hs_ref[...])


def gated_embedding(u, u_s, w_pg, p, p_s, w_pe, h, h_s, tm, tn):
    m, k = u.shape
    ms = u_s.shape[0]
    n = w_pg.shape[1]
    kp = p.shape[1]
    scol = _sample_col(n // tn)
    return pl.pallas_call(
        _ple_kernel,
        grid=(m // tm, n // tn),
        in_specs=[_row_resident((tm, k), lambda i, j: (i, 0)),
                  _row_resident((ms, k), lambda i, j: (0, 0)),
                  pl.BlockSpec((k, tn), lambda i, j: (0, j)),
                  pl.BlockSpec((tm, kp), lambda i, j: (i, 0)),
                  pl.BlockSpec((ms, kp), lambda i, j: (0, 0)),
                  pl.BlockSpec((kp, tn), lambda i, j: (0, j)),
                  pl.BlockSpec((tm, tn), lambda i, j: (i, j)),
                  pl.BlockSpec((ms, tn), scol)],
        out_specs=[pl.BlockSpec((tm, tn), lambda i, j: (i, j)),
                   pl.BlockSpec((ms, tn), scol)],
        out_shape=[jax.ShapeDtypeStruct((m, n), F32), jax.ShapeDtypeStruct((ms, n), F32)],
        compiler_params=_params("arbitrary", "arbitrary"),
        name="gated_embedding",
    )(u, u_s, w_pg, p, p_s, w_pe, h, h_s)


def kernel(x_prompt, x_sample, cache_k, cache_v, cache_logf, state_conv, page_table,
           p_prompt, p_sample, g_mix, w_in, b_f, q_gain, k_gain, conv_w,
           g_attn_out, g_conv_out, w_out, g_mlp, w_up, w_down, g_ple, w_pg, w_pe):
    assert w_in.shape[0] == 1, "single-layer trunk"
    batch, seq, _ = x_prompt.shape
    db = x_sample.shape[0]
    tm = 1024
    tr = 512

    w_in_t = jnp.transpose(w_in[0])
    b_fp = jnp.pad(b_f[0], (0, LANES - N_HEADS)).reshape(1, LANES)

    xp = x_prompt.reshape(batch * seq, D_MODEL)
    xs = x_sample.reshape(db, D_MODEL)
    u, lf = rmsnorm_forget(xp, g_mix[0], w_in_t, b_fp, tr)
    us, lfs = rmsnorm_forget(xs, g_mix[0], w_in_t, b_fp, db)
    (q, kf, kb, vf, vb, gb, ci), (qs, kfs, kbs, vfs, vbs, gbs, cis) = in_projection(
        u, us, w_in_t, q_gain[0], k_gain[0], tm)

    lfs_rep = jnp.broadcast_to(lfs[:, :N_HEADS, None], (db, N_HEADS, LANES))
    cache_lf_t = jnp.transpose(cache_logf[0], (0, 2, 1))
    aos = sample_attention(qs, kbs, vbs, lfs_rep, cache_k[0], cache_v[0], cache_lf_t, page_table)
    st = state_conv[0]
    os_ = mix_sample(aos, gbs, cis, st[:, 0, :], st[:, 1, :], conv_w[0], g_attn_out[0], g_conv_out[0])

    qx, kx = decay_features(lf, batch, seq)
    ao = prompt_attention(q, qx, kb, kx, vb, batch, seq, 256, 256)
    o = mix_prompt(ao, gb, ci, conv_w[0], g_attn_out[0], g_conv_out[0], seq, 256)

    h, hs = matmul_residual(o, os_, w_out[0], xp, xs, 2 * tm, 256)
    a, a_s = matmul_relu2(rmsnorm_bf16(h, g_mlp[0], tr), rmsnorm_bf16(hs, g_mlp[0], db),
                          w_up[0], 2 * tm, 256)
    h, hs = matmul_down_residual(a, a_s, w_down[0], h, hs, 2 * tm, 512, 2048)
    yp, ys = gated_embedding(rmsnorm_bf16(h, g_ple[0], tr), rmsnorm_bf16(hs, g_ple[0], db),
                             w_pg[0], p_prompt[0].reshape(batch * seq, PLE_DIM),
                             p_sample[0].reshape(db, PLE_DIM), w_pe[0], h, hs, tm, 512)

    k_prompt = kf.reshape(1, batch, seq, N_HEADS, HEAD_DIM)
    v_prompt = vf.reshape(1, batch, seq, N_HEADS, HEAD_DIM)
    logf_prompt = lf[:, :N_HEADS].reshape(1, batch, seq, N_HEADS)
    conv_prompt = ci.reshape(batch, seq, CONV_WIDTH)[None, :, seq - (CONV_K - 1):, :]
    k_sample = kfs.reshape(1, db, 1, N_HEADS, HEAD_DIM)
    v_sample = vfs.reshape(1, db, 1, N_HEADS, HEAD_DIM)
    logf_sample = lfs[:, :N_HEADS].reshape(1, db, 1, N_HEADS)
    conv_sample = jnp.stack([st[:, 1, :], cis], axis=1)[None]

    return (yp.reshape(batch, seq, D_MODEL), ys.reshape(db, 1, D_MODEL),
            k_prompt, v_prompt, logf_prompt, conv_prompt,
            k_sample, v_sample, logf_sample, conv_sample)
```

```python
import functools
import math

import jax
import jax.numpy as jnp
from jax import lax
from jax.experimental import pallas as pl
from jax.experimental.pallas import tpu as pltpu

F32 = jnp.float32
BF16 = jnp.bfloat16

D_MODEL = 4096
HEAD_DIM = 128
N_HEADS = 16
ATT_WIDTH = N_HEADS * HEAD_DIM
CONV_WIDTH = D_MODEL - ATT_WIDTH
CONV_K = 3
PLE_DIM = 256
PAGE_SIZE = 128
EPS = 1e-6
NEG_INF = -1e30
LOG2E = math.log2(math.e)
Q_SCALE = HEAD_DIM ** -0.5 * LOG2E
LANES = 128
HEAD_TILE = 6 * HEAD_DIM
VMEM_LIMIT = 60 * 1024 * 1024
HIGHEST = lax.Precision.HIGHEST
N_SPLIT = 3


def _params(*sem):
    return pltpu.CompilerParams(dimension_semantics=sem, vmem_limit_bytes=VMEM_LIMIT)


def _log_sigmoid(x):
    return jnp.minimum(x, 0.0) - jnp.log1p(jnp.exp(-jnp.abs(x)))


def _row_rms_scale(x):
    return lax.rsqrt(jnp.mean(x * x, axis=-1, keepdims=True) + EPS)


def _split3(x):
    hi = x.astype(BF16).astype(F32)
    r = x - hi
    mid = r.astype(BF16).astype(F32)
    lo = (r - mid).astype(BF16).astype(F32)
    return hi, mid, lo


def _dot(a, b):
    return jnp.dot(a, b, preferred_element_type=F32)


def _dot_nt(a, b):
    return lax.dot_general(a, b, (((1,), (1,)), ((), ())), preferred_element_type=F32)


def _rmsnorm_kernel(x_ref, g_ref, o_ref):
    x = x_ref[...]
    o_ref[...] = ((x * _row_rms_scale(x)) * g_ref[...]).astype(o_ref.dtype)


def rmsnorm_bf16(x, g, tm):
    m, d = x.shape
    return pl.pallas_call(
        _rmsnorm_kernel,
        grid=(m // tm,),
        in_specs=[pl.BlockSpec((tm, d), lambda i: (i, 0)),
                  pl.BlockSpec((1, d), lambda i: (0, 0))],
        out_specs=pl.BlockSpec((tm, d), lambda i: (i, 0)),
        out_shape=jax.ShapeDtypeStruct((m, d), BF16),
        compiler_params=_params("parallel"),
        name="rmsnorm_bf16",
    )(x, g.reshape(1, d))


def _rmsnorm_forget_kernel(x_ref, g_ref, wf_ref, bf_ref, u_ref, lf_ref):
    x = x_ref[...]
    u = ((x * _row_rms_scale(x)) * g_ref[...]).astype(BF16)
    u_ref[...] = u
    lf_ref[...] = _log_sigmoid(_dot_nt(u, wf_ref[...].astype(BF16)) + bf_ref[...])


def rmsnorm_forget(x, g, w_in_t, b_f, tm):
    m, d = x.shape
    fixed = lambda i: (0, 0)
    f_block = 3 * ATT_WIDTH // LANES
    return pl.pallas_call(
        _rmsnorm_forget_kernel,
        grid=(m // tm,),
        in_specs=[pl.BlockSpec((tm, d), lambda i: (i, 0)),
                  pl.BlockSpec((1, d), fixed),
                  pl.BlockSpec((LANES, d), lambda i: (f_block, 0)),
                  pl.BlockSpec((1, LANES), fixed)],
        out_specs=[pl.BlockSpec((tm, d), lambda i: (i, 0)),
                   pl.BlockSpec((tm, LANES), lambda i: (i, 0))],
        out_shape=[jax.ShapeDtypeStruct((m, d), BF16), jax.ShapeDtypeStruct((m, LANES), F32)],
        compiler_params=_params("parallel"),
        name="rmsnorm_forget",
    )(x, g.reshape(1, d), w_in_t, b_f)


def _head_epilogue(z, qg_ref, kg_ref, outs, rows):
    q_ref, kf_ref, kb_ref, vf_ref, vb_ref, gb_ref, ci_ref = outs
    d = HEAD_DIM
    zq, zk, zv = z[:, 0:d], z[:, d:2 * d], z[:, 2 * d:3 * d]
    q = (zq * _row_rms_scale(zq)) * qg_ref[...]
    k = (zk * _row_rms_scale(zk)) * kg_ref[...]
    q_ref[rows, :] = (q * Q_SCALE).astype(BF16)
    kf_ref[rows, :] = k
    kb_ref[rows, :] = k.astype(BF16)
    vf_ref[rows, :] = zv
    vb_ref[rows, :] = zv.astype(BF16)
    gb_ref[rows, :] = z[:, 3 * d:4 * d]
    ci_ref[rows, :] = z[:, 4 * d:5 * d] * z[:, 5 * d:6 * d]


def _inproj_kernel(u_ref, us_ref, wq_ref, wk_ref, wv_ref, wb_ref, wc_ref, wh_ref, qg_ref, kg_ref,
                   *refs, tm, rc):
    outs, outs_s, w_scr = refs[0:7], refs[7:14], refs[14]
    d = HEAD_DIM

    @pl.when(pl.program_id(1) == 0)
    def _():
        for s, w_ref in enumerate((wq_ref, wk_ref, wv_ref, wb_ref, wc_ref, wh_ref)):
            w_scr[:, s * d:(s + 1) * d] = w_ref[...].T.astype(BF16)
        _head_epilogue(_dot(us_ref[...], w_scr[...]), qg_ref, kg_ref, outs_s, slice(None))

    for r in range(tm // rc):
        rows = slice(r * rc, (r + 1) * rc)
        _head_epilogue(_dot(u_ref[rows, :], w_scr[...]), qg_ref, kg_ref, outs, rows)


def in_projection(u, us, w_in_t, q_gain, k_gain, tm):
    m, dm = u.shape
    ms = us.shape[0]
    nh = N_HEADS
    fixed = lambda h, i: (0, 0)
    conv0 = 3 * ATT_WIDTH + nh

    def att_rows(s):
        return pl.BlockSpec((HEAD_DIM, dm), lambda h, i: (s * nh + h, 0))

    def conv_rows(s):
        return pl.BlockSpec((pl.Element(HEAD_DIM), pl.Element(dm)),
                            lambda h, i: (pl.multiple_of(conv0 + (s * nh + h) * HEAD_DIM, 8), 0))

    head_out = pl.BlockSpec((tm, HEAD_DIM), lambda h, i: (i, h))
    head_out_s = pl.BlockSpec((ms, HEAD_DIM), lambda h, i: (0, h))
    dts = (BF16, F32, BF16, F32, BF16, F32, F32)
    shapes = [jax.ShapeDtypeStruct((m, ATT_WIDTH), dt) for dt in dts]
    shapes += [jax.ShapeDtypeStruct((ms, ATT_WIDTH), dt) for dt in dts]
    outs = pl.pallas_call(
        functools.partial(_inproj_kernel, tm=tm, rc=256),
        grid=(nh, m // tm),
        in_specs=[pl.BlockSpec((tm, dm), lambda h, i: (i, 0)),
                  pl.BlockSpec((ms, dm), fixed),
                  att_rows(0), att_rows(1), att_rows(2), conv_rows(0), conv_rows(1), conv_rows(2),
                  pl.BlockSpec((1, HEAD_DIM), fixed),
                  pl.BlockSpec((1, HEAD_DIM), fixed)],
        out_specs=[head_out] * 7 + [head_out_s] * 7,
        out_shape=shapes,
        scratch_shapes=[pltpu.VMEM((dm, HEAD_TILE), BF16)],
        compiler_params=_params("parallel", "arbitrary"),
        name="in_projection",
    )(u, us, *([w_in_t] * 6), q_gain.reshape(1, HEAD_DIM), k_gain.reshape(1, HEAD_DIM))
    return outs[0:7], outs[7:14]


def _decay_features_kernel(lf_ref, qx_ref, kx_ref, *, t):
    r = lax.broadcasted_iota(jnp.int32, (LANES, LANES), 0)
    c = lax.broadcasted_iota(jnp.int32, (LANES, LANES), 1)
    tri = (c <= r).astype(F32)
    lane = c
    carry = jnp.zeros((1, LANES), F32)
    for j in range(t // LANES):
        rows = slice(j * LANES, (j + 1) * LANES)
        cs = jnp.dot(tri, lf_ref[rows, :], precision=HIGHEST, preferred_element_type=F32) + carry
        carry = cs[LANES - 1:LANES, :]
        pieces = _split3(cs * LOG2E)
        for h in range(N_HEADS):
            cols = slice(h * HEAD_DIM, (h + 1) * HEAD_DIM)
            qx = jnp.where((lane >= N_SPLIT) & (lane < 2 * N_SPLIT), 1.0, 0.0)
            kx = jnp.where(lane < N_SPLIT, 1.0, 0.0)
            for s, piece in enumerate(pieces):
                col = piece[:, h:h + 1]
                qx = jnp.where(lane == s, col, qx)
                kx = jnp.where(lane == N_SPLIT + s, -col, kx)
            qx_ref[rows, cols] = qx.astype(BF16)
            kx_ref[rows, cols] = kx.astype(BF16)


def decay_features(lf, batch, t):
    m = lf.shape[0]
    out = pl.BlockSpec((t, ATT_WIDTH), lambda b: (b, 0))
    return pl.pallas_call(
        functools.partial(_decay_features_kernel, t=t),
        grid=(batch,),
        in_specs=[pl.BlockSpec((t, LANES), lambda b: (b, 0))],
        out_specs=[out, out],
        out_shape=[jax.ShapeDtypeStruct((m, ATT_WIDTH), BF16)] * 2,
        compiler_params=_params("parallel"),
        name="decay_features",
    )(lf)


def _flash_kernel(q_ref, qx_ref, k_ref, kx_ref, v_ref, o_ref, m_scr, l_scr, acc_scr, *, tq, tk):
    qi = pl.program_id(1)
    kj = pl.program_id(2)
    d = HEAD_DIM
    last = (qi * tq) // tk

    @pl.when(kj == 0)
    def _():
        m_scr[...] = jnp.full(m_scr.shape, NEG_INF, F32)
        l_scr[...] = jnp.zeros(l_scr.shape, F32)
        acc_scr[...] = jnp.zeros(acc_scr.shape, F32)

    def step(masked):
        if masked:
            row = lax.broadcasted_iota(jnp.int32, (tq, tk), 0) + (qi * tq - kj * tk)
            col = lax.broadcasted_iota(jnp.int32, (tq, tk), 1)
            keep = col <= row
        for h in range(N_HEADS):
            cs = slice(h * d, (h + 1) * d)
            qa = jnp.concatenate([q_ref[:, cs], qx_ref[:, cs]], axis=1)
            ka = jnp.concatenate([k_ref[:, cs], kx_ref[:, cs]], axis=1)
            s = lax.dot_general(qa, ka, (((1,), (1,)), ((), ())), preferred_element_type=F32)
            if masked:
                s = jnp.where(keep, s, NEG_INF)
            m_old = m_scr[h]
            m_new = jnp.maximum(m_old, jnp.max(s, axis=-1, keepdims=True))
            p = jnp.exp2(s - pltpu.repeat(m_new, tk // LANES, axis=1))
            alpha = jnp.exp2(m_old - m_new)
            l_scr[h] = alpha * l_scr[h] + jnp.sum(p, axis=-1, keepdims=True)
            m_scr[h] = m_new
            acc_scr[:, cs] = alpha * acc_scr[:, cs] + _dot(p.astype(BF16), v_ref[:, cs])

    @pl.when(kj < last)
    def _():
        step(False)

    @pl.when(kj == last)
    def _():
        step(True)
        for h in range(N_HEADS):
            cs = slice(h * d, (h + 1) * d)
            o_ref[:, cs] = acc_scr[:, cs] / l_scr[h]


def prompt_attention(q, qx, k, kx, v, batch, t, tq, tk):
    m = q.shape[0]
    nq, nk = t // tq, t // tk
    qrow = lambda b, i, j: (b * nq + i, 0)
    krow = lambda b, i, j: (b * nk + jnp.minimum(j, (i * tq) // tk), 0)
    qspec = pl.BlockSpec((tq, ATT_WIDTH), qrow)
    kspec = pl.BlockSpec((tk, ATT_WIDTH), krow)
    return pl.pallas_call(
        functools.partial(_flash_kernel, tq=tq, tk=tk),
        grid=(batch, nq, nk),
        in_specs=[qspec, qspec, kspec, kspec, kspec],
        out_specs=qspec,
        out_shape=jax.ShapeDtypeStruct((m, ATT_WIDTH), F32),
        scratch_shapes=[pltpu.VMEM((N_HEADS, tq, LANES), F32),
                        pltpu.VMEM((N_HEADS, tq, LANES), F32),
                        pltpu.VMEM((tq, ATT_WIDTH), F32)],
        compiler_params=_params("parallel", "parallel", "arbitrary"),
        name="prompt_attention",
    )(q, qx, k, kx, v)


PAGES_PER_STEP = 8


def _decode_kernel(pt_ref, q_ref, kn_ref, vn_ref, lfn_ref, *refs, n_steps):
    del pt_ref
    npg = PAGES_PER_STEP
    k_refs, v_refs, lp_refs = refs[0:npg], refs[npg:2 * npg], refs[2 * npg:3 * npg]
    o_ref, s_scr, piece_scr, m_scr, l_scr, tot_scr, acc_scr = refs[3 * npg:]
    s_id = pl.program_id(1)
    ones = jnp.ones((2 * HEAD_DIM, LANES), BF16)
    qs = q_ref[...].astype(F32)

    @pl.when(s_id == 0)
    def _():
        prod = (kn_ref[...].astype(F32) * qs).astype(BF16)
        m_scr[...] = _dot(prod, ones[0:HEAD_DIM, :])
        l_scr[...] = jnp.ones(l_scr.shape, F32)
        acc_scr[...] = vn_ref[...].astype(F32)
        tot_scr[...] = lfn_ref[...]
        piece_scr[...] = jnp.zeros(piece_scr.shape, F32)

    r = lax.broadcasted_iota(jnp.int32, (PAGE_SIZE, PAGE_SIZE), 0)
    c = lax.broadcasted_iota(jnp.int32, (PAGE_SIZE, PAGE_SIZE), 1)
    later_t = (r > c).astype(F32)
    hh = lax.broadcasted_iota(jnp.int32, (N_HEADS, LANES), 0)
    ll = lax.broadcasted_iota(jnp.int32, (N_HEADS, LANES), 1)
    own = ((ll % N_HEADS) == hh) & (ll < N_SPLIT * N_HEADS)

    tot = tot_scr[...]
    for j in range(npg):
        lp_t = lp_refs[j][...]
        dec_t = jnp.dot(lp_t, later_t, precision=HIGHEST, preferred_element_type=F32) + tot
        tot = tot + jnp.sum(lp_t, axis=1, keepdims=True)
        for s, piece in enumerate(_split3(dec_t * LOG2E)):
            piece_scr[j, s * N_HEADS:(s + 1) * N_HEADS, :] = piece
        pieces = piece_scr[j].T
        spread = jnp.broadcast_to(pieces[:, None, :], (PAGE_SIZE, N_HEADS, LANES))
        extra = jnp.where(own[None], spread, 0.0).astype(BF16).reshape(PAGE_SIZE * N_HEADS, LANES)
        prod = (k_refs[j][...] * qs[None]).astype(BF16).reshape(PAGE_SIZE * N_HEADS, HEAD_DIM)
        s_scr[j] = _dot(jnp.concatenate([prod, extra], axis=1), ones)

    m = m_scr[...]
    l = l_scr[...]
    acc = acc_scr[...]
    for j in range(npg):
        s2 = s_scr[j].reshape(PAGE_SIZE, N_HEADS, LANES)
        m_new = jnp.maximum(m, jnp.max(s2, axis=0))
        alpha = jnp.exp2(m - m_new)
        p = jnp.exp2(s2 - m_new[None])
        l = alpha * l + jnp.sum(p, axis=0)
        acc = alpha * acc + jnp.sum(p * v_refs[j][...], axis=0)
        m = m_new
    tot_scr[...] = tot
    m_scr[...] = m
    l_scr[...] = l
    acc_scr[...] = acc

    @pl.when(s_id == n_steps - 1)
    def _():
        o_ref[...] = acc / l


def sample_attention(q, k_new, v_new, lf_new, cache_k, cache_v, cache_lf_t, page_table):
    db, n_pages = page_table.shape
    npg = PAGES_PER_STEP
    n_steps = n_pages // npg
    heads = pl.BlockSpec((None, N_HEADS, HEAD_DIM), lambda b, s, pt: (b, 0, 0))

    def page(j):
        return lambda b, s, pt: (pt[b, n_pages - 1 - (npg * s + j)], 0, 0, 0)

    def lf_page(j):
        return lambda b, s, pt: (pt[b, n_pages - 1 - (npg * s + j)], 0, 0)

    kv_specs = [pl.BlockSpec((None, PAGE_SIZE, N_HEADS, HEAD_DIM), page(j)) for j in range(npg)]
    lp_specs = [pl.BlockSpec((None, N_HEADS, PAGE_SIZE), lf_page(j)) for j in range(npg)]
    stat = pltpu.VMEM((N_HEADS, LANES), F32)
    grid_spec = pltpu.PrefetchScalarGridSpec(
        num_scalar_prefetch=1,
        grid=(db, n_steps),
        in_specs=[heads, heads, heads, heads] + kv_specs + kv_specs + lp_specs,
        out_specs=heads,
        scratch_shapes=[pltpu.VMEM((npg, PAGE_SIZE * N_HEADS, LANES), F32),
                        pltpu.VMEM((npg, PAGE_SIZE, LANES), F32), stat, stat, stat, stat],
    )
    out = pl.pallas_call(
        functools.partial(_decode_kernel, n_steps=n_steps),
        grid_spec=grid_spec,
        out_shape=jax.ShapeDtypeStruct((db, N_HEADS, HEAD_DIM), F32),
        compiler_params=_params("parallel", "arbitrary"),
        name="sample_attention",
    )(page_table, q.reshape(db, N_HEADS, HEAD_DIM), k_new.reshape(db, N_HEADS, HEAD_DIM),
      v_new.reshape(db, N_HEADS, HEAD_DIM), lf_new,
      *([cache_k] * npg), *([cache_v] * npg), *([cache_lf_t] * npg))
    return out.reshape(db, ATT_WIDTH)


def _mix_store(ao, conv, gb, gao_ref, gco_ref, o_ref):
    conv_o = gb * conv
    o_ref[:, 0:ATT_WIDTH] = ((ao * _row_rms_scale(ao)) * gao_ref[...]).astype(BF16)
    o_ref[:, ATT_WIDTH:D_MODEL] = ((conv_o * _row_rms_scale(conv_o)) * gco_ref[...]).astype(BF16)


def _mix_prompt_kernel(ao_ref, gb_ref, ci_ref, halo_ref, w_ref, gao_ref, gco_ref, o_ref, *,
                       tm, tiles_per_seq):
    ci = ci_ref[...]
    first = pl.program_id(0) % tiles_per_seq == 0
    halo = jnp.where(first, 0.0, halo_ref[...])
    row = lax.broadcasted_iota(jnp.int32, ci.shape, 0)
    prev1 = jnp.where(row == 0, halo[7:8, :], pltpu.roll(ci, 1, axis=0))
    prev2 = jnp.where(row == 0, halo[6:7, :],
                      jnp.where(row == 1, halo[7:8, :], pltpu.roll(ci, 2, axis=0)))
    conv = w_ref[0:1, :] * prev2 + w_ref[1:2, :] * prev1 + w_ref[2:3, :] * ci
    _mix_store(ao_ref[...], conv, gb_ref[...], gao_ref, gco_ref, o_ref)


def mix_prompt(ao, gb, ci, conv_w, g_ao, g_co, t, tm):
    m = ao.shape[0]
    rows = lambda i: (i, 0)
    fixed = lambda i: (0, 0)
    tile = pl.BlockSpec((tm, ATT_WIDTH), rows)
    halo = pl.BlockSpec((8, CONV_WIDTH), lambda i: (jnp.maximum(i * (tm // 8) - 1, 0), 0))
    return pl.pallas_call(
        functools.partial(_mix_prompt_kernel, tm=tm, tiles_per_seq=t // tm),
        grid=(m // tm,),
        in_specs=[tile, tile, tile, halo,
                  pl.BlockSpec((CONV_K, CONV_WIDTH), fixed),
                  pl.BlockSpec((1, ATT_WIDTH), fixed),
                  pl.BlockSpec((1, CONV_WIDTH), fixed)],
        out_specs=pl.BlockSpec((tm, D_MODEL), rows),
        out_shape=jax.ShapeDtypeStruct((m, D_MODEL), BF16),
        compiler_params=_params("parallel"),
        name="mix_prompt",
    )(ao, gb, ci, ci, conv_w, g_ao.reshape(1, -1), g_co.reshape(1, -1))


def _mix_sample_kernel(ao_ref, gb_ref, ci_ref, s0_ref, s1_ref, w_ref, gao_ref, gco_ref, o_ref):
    conv = w_ref[0:1, :] * s0_ref[...] + w_ref[1:2, :] * s1_ref[...] + w_ref[2:3, :] * ci_ref[...]
    _mix_store(ao_ref[...], conv, gb_ref[...], gao_ref, gco_ref, o_ref)


def mix_sample(ao, gb, ci, s0, s1, conv_w, g_ao, g_co):
    m = ao.shape[0]
    full = lambda shape: pl.BlockSpec(shape, lambda i: (0, 0))
    tile = full((m, ATT_WIDTH))
    return pl.pallas_call(
        _mix_sample_kernel,
        grid=(1,),
        in_specs=[tile] * 5 + [full((CONV_K, CONV_WIDTH)), full((1, ATT_WIDTH)), full((1, CONV_WIDTH))],
        out_specs=full((m, D_MODEL)),
        out_shape=jax.ShapeDtypeStruct((m, D_MODEL), BF16),
        compiler_params=_params("arbitrary"),
        name="mix_sample",
    )(ao, gb, ci, s0, s1, conv_w, g_ao.reshape(1, -1), g_co.reshape(1, -1))


def _sample_col(nj):
    return lambda i, j, *_: (0, jnp.where(i == 0, j, nj - 1))


def _first_row_tile():
    return pl.program_id(0) == 0


def _row_resident(shape, index_map):
    return pl.BlockSpec(shape, index_map, pipeline_mode=pl.Buffered(1))


def _mm_resid_kernel(a_ref, as_ref, w_ref, r_ref, rs_ref, o_ref, os_ref):
    o_ref[...] = r_ref[...] + _dot(a_ref[...], w_ref[...].astype(BF16))

    @pl.when(_first_row_tile())
    def _():
        os_ref[...] = rs_ref[...] + _dot(as_ref[...], w_ref[...].astype(BF16))


def matmul_residual(a, a_s, w, resid, resid_s, tm, tn):
    m, k = a.shape
    ms = a_s.shape[0]
    n = w.shape[1]
    scol = _sample_col(n // tn)
    return pl.pallas_call(
        _mm_resid_kernel,
        grid=(m // tm, n // tn),
        in_specs=[pl.BlockSpec((tm, k), lambda i, j: (i, 0)),
                  _row_resident((ms, k), lambda i, j: (0, 0)),
                  pl.BlockSpec((k, tn), lambda i, j: (0, j)),
                  pl.BlockSpec((tm, tn), lambda i, j: (i, j)),
                  pl.BlockSpec((ms, tn), scol)],
        out_specs=[pl.BlockSpec((tm, tn), lambda i, j: (i, j)),
                   pl.BlockSpec((ms, tn), scol)],
        out_shape=[jax.ShapeDtypeStruct((m, n), F32), jax.ShapeDtypeStruct((ms, n), F32)],
        compiler_params=_params("arbitrary", "arbitrary"),
        name="matmul_residual",
    )(a, a_s, w, resid, resid_s)


def _relu2(z):
    z = jnp.maximum(z, 0.0)
    return (z * z).astype(BF16)


def _mm_relu2_kernel(a_ref, as_ref, w_ref, o_ref, os_ref):
    o_ref[...] = _relu2(_dot(a_ref[...], w_ref[...].astype(BF16)))

    @pl.when(_first_row_tile())
    def _():
        os_ref[...] = _relu2(_dot(as_ref[...], w_ref[...].astype(BF16)))


def matmul_relu2(a, a_s, w, tm, tn):
    m, k = a.shape
    ms = a_s.shape[0]
    n = w.shape[1]
    scol = _sample_col(n // tn)
    return pl.pallas_call(
        _mm_relu2_kernel,
        grid=(m // tm, n // tn),
        in_specs=[pl.BlockSpec((tm, k), lambda i, j: (i, 0)),
                  _row_resident((ms, k), lambda i, j: (0, 0)),
                  pl.BlockSpec((k, tn), lambda i, j: (0, j))],
        out_specs=[pl.BlockSpec((tm, tn), lambda i, j: (i, j)),
                   pl.BlockSpec((ms, tn), scol)],
        out_shape=[jax.ShapeDtypeStruct((m, n), BF16), jax.ShapeDtypeStruct((ms, n), BF16)],
        compiler_params=_params("arbitrary", "arbitrary"),
        name="matmul_relu2",
    )(a, a_s, w)


def _mm_down_kernel(a_ref, as_ref, w_ref, r_ref, rs_ref, o_ref, os_ref, acc_ref, accs_ref, *, nk):
    kk = pl.program_id(2)

    @pl.when(kk == 0)
    def _():
        acc_ref[...] = r_ref[...]

    acc_ref[...] += _dot(a_ref[...], w_ref[...].astype(BF16))

    @pl.when(kk == nk - 1)
    def _():
        o_ref[...] = acc_ref[...]

    @pl.when(_first_row_tile())
    def _():
        @pl.when(kk == 0)
        def _():
            accs_ref[...] = rs_ref[...]

        accs_ref[...] += _dot(as_ref[...], w_ref[...].astype(BF16))

        @pl.when(kk == nk - 1)
        def _():
            os_ref[...] = accs_ref[...]


def matmul_down_residual(a, a_s, w, resid, resid_s, tm, tn, tk):
    m, k = a.shape
    ms = a_s.shape[0]
    n = w.shape[1]
    nk = k // tk
    scol = _sample_col(n // tn)
    return pl.pallas_call(
        functools.partial(_mm_down_kernel, nk=nk),
        grid=(m // tm, n // tn, nk),
        in_specs=[pl.BlockSpec((tm, tk), lambda i, j, kk: (i, kk)),
                  pl.BlockSpec((ms, tk), lambda i, j, kk: (0, jnp.where(i == 0, kk, nk - 1))),
                  pl.BlockSpec((tk, tn), lambda i, j, kk: (kk, j)),
                  pl.BlockSpec((tm, tn), lambda i, j, kk: (i, j)),
                  pl.BlockSpec((ms, tn), scol)],
        out_specs=[pl.BlockSpec((tm, tn), lambda i, j, kk: (i, j)),
                   pl.BlockSpec((ms, tn), scol)],
        out_shape=[jax.ShapeDtypeStruct((m, n), F32), jax.ShapeDtypeStruct((ms, n), F32)],
        scratch_shapes=[pltpu.VMEM((tm, tn), F32), pltpu.VMEM((ms, tn), F32)],
        compiler_params=_params("arbitrary", "arbitrary", "arbitrary"),
        name="matmul_down_residual",
    )(a, a_s, w, resid, resid_s)


def _gated(u, wg, p, we, h):
    gate = jax.nn.sigmoid(_dot(u, wg.astype(BF16)))
    return h + gate * _dot(p.astype(BF16), we.astype(BF16))


def _ple_kernel(u_ref, us_ref, wg_ref, p_ref, ps_ref, we_ref, h_ref, hs_ref, o_ref, os_ref):
    o_ref[...] = _gated(u_ref[...], wg_ref[...], p_ref[...], we_ref[...], h_ref[...])

    @pl.when(_first_row_tile())
    def _():
        os_ref[...] = _gated(us_ref[...], wg_ref[...], ps_ref[...], we_ref[...], hs_ref[...])


def gated_embedding(u, u_s, w_pg, p, p_s, w_pe, h, h_s, tm, tn):
    m, k = u.shape
    ms = u_s.shape[0]
    n = w_pg.shape[1]
    kp = p.shape[1]
    scol = _sample_col(n // tn)
    return pl.pallas_call(
        _ple_kernel,
        grid=(m // tm, n // tn),
        in_specs=[pl.BlockSpec((tm, k), lambda i, j: (i, 0)),
                  _row_resident((ms, k), lambda i, j: (0, 0)),
                  pl.BlockSpec((k, tn), lambda i, j: (0, j)),
                  pl.BlockSpec((tm, kp), lambda i, j: (i, 0)),
                  pl.BlockSpec((ms, kp), lambda i, j: (0, 0)),
                  pl.BlockSpec((kp, tn), lambda i, j: (0, j)),
                  pl.BlockSpec((tm, tn), lambda i, j: (i, j)),
                  pl.BlockSpec((ms, tn), scol)],
        out_specs=[pl.BlockSpec((tm, tn), lambda i, j: (i, j)),
                   pl.BlockSpec((ms, tn), scol)],
        out_shape=[jax.ShapeDtypeStruct((m, n), F32), jax.ShapeDtypeStruct((ms, n), F32)],
        compiler_params=_params("arbitrary", "arbitrary"),
        name="gated_embedding",
    )(u, u_s, w_pg, p, p_s, w_pe, h, h_s)


def kernel(x_prompt, x_sample, cache_k, cache_v, cache_logf, state_conv, page_table,
           p_prompt, p_sample, g_mix, w_in, b_f, q_gain, k_gain, conv_w,
           g_attn_out, g_conv_out, w_out, g_mlp, w_up, w_down, g_ple, w_pg, w_pe):
    assert w_in.shape[0] == 1, "single-layer trunk"
    batch, seq, _ = x_prompt.shape
    db = x_sample.shape[0]
    tm = 1024
    tr = 512

    w_in_t = jnp.transpose(w_in[0])
    b_fp = jnp.pad(b_f[0], (0, LANES - N_HEADS)).reshape(1, LANES)

    xp = x_prompt.reshape(batch * seq, D_MODEL)
    xs = x_sample.reshape(db, D_MODEL)
    u, lf = rmsnorm_forget(xp, g_mix[0], w_in_t, b_fp, tr)
    us, lfs = rmsnorm_forget(xs, g_mix[0], w_in_t, b_fp, db)
    (q, kf, kb, vf, vb, gb, ci), (qs, kfs, kbs, vfs, vbs, gbs, cis) = in_projection(
        u, us, w_in_t, q_gain[0], k_gain[0], tm)

    lfs_rep = jnp.broadcast_to(lfs[:, :N_HEADS, None], (db, N_HEADS, LANES))
    cache_lf_t = jnp.transpose(cache_logf[0], (0, 2, 1))
    aos = sample_attention(qs, kbs, vbs, lfs_rep, cache_k[0], cache_v[0], cache_lf_t, page_table)
    st = state_conv[0]
    os_ = mix_sample(aos, gbs, cis, st[:, 0, :], st[:, 1, :], conv_w[0], g_attn_out[0], g_conv_out[0])

    qx, kx = decay_features(lf, batch, seq)
    ao = prompt_attention(q, qx, kb, kx, vb, batch, seq, 256, 256)
    o = mix_prompt(ao, gb, ci, conv_w[0], g_attn_out[0], g_conv_out[0], seq, 256)

    h, hs = matmul_residual(o, os_, w_out[0], xp, xs, tm, 512)
    a, a_s = matmul_relu2(rmsnorm_bf16(h, g_mlp[0], tr), rmsnorm_bf16(hs, g_mlp[0], db),
                          w_up[0], tm, 512)
    h, hs = matmul_down_residual(a, a_s, w_down[0], h, hs, tm, 1024, 2048)
    yp, ys = gated_embedding(rmsnorm_bf16(h, g_ple[0], tr), rmsnorm_bf16(hs, g_ple[0], db),
                             w_pg[0], p_prompt[0].reshape(batch * seq, PLE_DIM),
                             p_sample[0].reshape(db, PLE_DIM), w_pe[0], h, hs, tm, 512)

    k_prompt = kf.reshape(1, batch, seq, N_HEADS, HEAD_DIM)
    v_prompt = vf.reshape(1, batch, seq, N_HEADS, HEAD_DIM)
    logf_prompt = lf[:, :N_HEADS].reshape(1, batch, seq, N_HEADS)
    conv_prompt = ci.reshape(batch, seq, CONV_WIDTH)[None, :, seq - (CONV_K - 1):, :]
    k_sample = kfs.reshape(1, db, 1, N_HEADS, HEAD_DIM)
    v_sample = vfs.reshape(1, db, 1, N_HEADS, HEAD_DIM)
    logf_sample = lfs[:, :N_HEADS].reshape(1, db, 1, N_HEADS)
    conv_sample = jnp.stack([st[:, 1, :], cis], axis=1)[None]

    return (yp.reshape(batch, seq, D_MODEL), ys.reshape(db, 1, D_MODEL),
            k_prompt, v_prompt, logf_prompt, conv_prompt,
            k_sample, v_sample, logf_sample, conv_sample)
```

```python
import functools
import math

import jax
import jax.numpy as jnp
from jax import lax
from jax.experimental import pallas as pl
from jax.experimental.pallas import tpu as pltpu

F32 = jnp.float32
BF16 = jnp.bfloat16

D_MODEL = 4096
HEAD_DIM = 128
N_HEADS = 16
ATT_WIDTH = N_HEADS * HEAD_DIM
CONV_WIDTH = D_MODEL - ATT_WIDTH
CONV_K = 3
PLE_DIM = 256
PAGE_SIZE = 128
EPS = 1e-6
NEG_INF = -1e30
LOG2E = math.log2(math.e)
Q_SCALE = HEAD_DIM ** -0.5 * LOG2E
LANES = 128
HEAD_TILE = 6 * HEAD_DIM
VMEM_LIMIT = 60 * 1024 * 1024
HIGHEST = lax.Precision.HIGHEST
N_SPLIT = 3


def _params(*sem):
    return pltpu.CompilerParams(dimension_semantics=sem, vmem_limit_bytes=VMEM_LIMIT)


def _log_sigmoid(x):
    return jnp.minimum(x, 0.0) - jnp.log1p(jnp.exp(-jnp.abs(x)))


def _row_rms_scale(x):
    return lax.rsqrt(jnp.mean(x * x, axis=-1, keepdims=True) + EPS)


def _split3(x):
    hi = x.astype(BF16).astype(F32)
    r = x - hi
    mid = r.astype(BF16).astype(F32)
    lo = (r - mid).astype(BF16).astype(F32)
    return hi, mid, lo


def _dot(a, b):
    return jnp.dot(a, b, preferred_element_type=F32)


def _dot_nt(a, b):
    return lax.dot_general(a, b, (((1,), (1,)), ((), ())), preferred_element_type=F32)


def _rmsnorm_kernel(x_ref, g_ref, o_ref):
    x = x_ref[...]
    o_ref[...] = ((x * _row_rms_scale(x)) * g_ref[...]).astype(o_ref.dtype)


def rmsnorm_bf16(x, g, tm):
    m, d = x.shape
    return pl.pallas_call(
        _rmsnorm_kernel,
        grid=(m // tm,),
        in_specs=[pl.BlockSpec((tm, d), lambda i: (i, 0)),
                  pl.BlockSpec((1, d), lambda i: (0, 0))],
        out_specs=pl.BlockSpec((tm, d), lambda i: (i, 0)),
        out_shape=jax.ShapeDtypeStruct((m, d), BF16),
        compiler_params=_params("parallel"),
        name="rmsnorm_bf16",
    )(x, g.reshape(1, d))


def _rmsnorm_forget_kernel(x_ref, g_ref, wf_ref, bf_ref, u_ref, lf_ref):
    x = x_ref[...]
    u = ((x * _row_rms_scale(x)) * g_ref[...]).astype(BF16)
    u_ref[...] = u
    lf_ref[...] = _log_sigmoid(_dot_nt(u, wf_ref[...].astype(BF16)) + bf_ref[...])


def rmsnorm_forget(x, g, w_in_t, b_f, tm):
    m, d = x.shape
    fixed = lambda i: (0, 0)
    f_block = 3 * ATT_WIDTH // LANES
    return pl.pallas_call(
        _rmsnorm_forget_kernel,
        grid=(m // tm,),
        in_specs=[pl.BlockSpec((tm, d), lambda i: (i, 0)),
                  pl.BlockSpec((1, d), fixed),
                  pl.BlockSpec((LANES, d), lambda i: (f_block, 0)),
                  pl.BlockSpec((1, LANES), fixed)],
        out_specs=[pl.BlockSpec((tm, d), lambda i: (i, 0)),
                   pl.BlockSpec((tm, LANES), lambda i: (i, 0))],
        out_shape=[jax.ShapeDtypeStruct((m, d), BF16), jax.ShapeDtypeStruct((m, LANES), F32)],
        compiler_params=_params("parallel"),
        name="rmsnorm_forget",
    )(x, g.reshape(1, d), w_in_t, b_f)


def _head_epilogue(z, qg_ref, kg_ref, outs, rows):
    q_ref, kf_ref, kb_ref, vf_ref, vb_ref, gb_ref, ci_ref = outs
    d = HEAD_DIM
    zq, zk, zv = z[:, 0:d], z[:, d:2 * d], z[:, 2 * d:3 * d]
    q = (zq * _row_rms_scale(zq)) * qg_ref[...]
    k = (zk * _row_rms_scale(zk)) * kg_ref[...]
    q_ref[rows, :] = (q * Q_SCALE).astype(BF16)
    kf_ref[rows, :] = k
    kb_ref[rows, :] = k.astype(BF16)
    vf_ref[rows, :] = zv
    vb_ref[rows, :] = zv.astype(BF16)
    gb_ref[rows, :] = z[:, 3 * d:4 * d]
    ci_ref[rows, :] = z[:, 4 * d:5 * d] * z[:, 5 * d:6 * d]


def _inproj_kernel(u_ref, us_ref, wq_ref, wk_ref, wv_ref, wb_ref, wc_ref, wh_ref, qg_ref, kg_ref,
                   *refs, tm, rc):
    outs, outs_s, w_scr = refs[0:7], refs[7:14], refs[14]
    d = HEAD_DIM

    @pl.when(pl.program_id(1) == 0)
    def _():
        for s, w_ref in enumerate((wq_ref, wk_ref, wv_ref, wb_ref, wc_ref, wh_ref)):
            w_scr[:, s * d:(s + 1) * d] = w_ref[...].T.astype(BF16)
        _head_epilogue(_dot(us_ref[...], w_scr[...]), qg_ref, kg_ref, outs_s, slice(None))

    for r in range(tm // rc):
        rows = slice(r * rc, (r + 1) * rc)
        _head_epilogue(_dot(u_ref[rows, :], w_scr[...]), qg_ref, kg_ref, outs, rows)


def in_projection(u, us, w_in_t, q_gain, k_gain, tm):
    m, dm = u.shape
    ms = us.shape[0]
    nh = N_HEADS
    fixed = lambda h, i: (0, 0)
    conv0 = 3 * ATT_WIDTH + nh

    def att_rows(s):
        return pl.BlockSpec((HEAD_DIM, dm), lambda h, i: (s * nh + h, 0))

    def conv_rows(s):
        return pl.BlockSpec((pl.Element(HEAD_DIM), pl.Element(dm)),
                            lambda h, i: (pl.multiple_of(conv0 + (s * nh + h) * HEAD_DIM, 8), 0))

    head_out = pl.BlockSpec((tm, HEAD_DIM), lambda h, i: (i, h))
    head_out_s = pl.BlockSpec((ms, HEAD_DIM), lambda h, i: (0, h))
    dts = (BF16, F32, BF16, F32, BF16, F32, F32)
    shapes = [jax.ShapeDtypeStruct((m, ATT_WIDTH), dt) for dt in dts]
    shapes += [jax.ShapeDtypeStruct((ms, ATT_WIDTH), dt) for dt in dts]
    outs = pl.pallas_call(
        functools.partial(_inproj_kernel, tm=tm, rc=256),
        grid=(nh, m // tm),
        in_specs=[pl.BlockSpec((tm, dm), lambda h, i: (i, 0)),
                  pl.BlockSpec((ms, dm), fixed),
                  att_rows(0), att_rows(1), att_rows(2), conv_rows(0), conv_rows(1), conv_rows(2),
                  pl.BlockSpec((1, HEAD_DIM), fixed),
                  pl.BlockSpec((1, HEAD_DIM), fixed)],
        out_specs=[head_out] * 7 + [head_out_s] * 7,
        out_shape=shapes,
        scratch_shapes=[pltpu.VMEM((dm, HEAD_TILE), BF16)],
        compiler_params=_params("parallel", "arbitrary"),
        name="in_projection",
    )(u, us, *([w_in_t] * 6), q_gain.reshape(1, HEAD_DIM), k_gain.reshape(1, HEAD_DIM))
    return outs[0:7], outs[7:14]


def _decay_features_kernel(lf_ref, qx_ref, kx_ref, *, t):
    r = lax.broadcasted_iota(jnp.int32, (LANES, LANES), 0)
    c = lax.broadcasted_iota(jnp.int32, (LANES, LANES), 1)
    tri = (c <= r).astype(F32)
    lane = c
    carry = jnp.zeros((1, LANES), F32)
    for j in range(t // LANES):
        rows = slice(j * LANES, (j + 1) * LANES)
        cs = jnp.dot(tri, lf_ref[rows, :], precision=HIGHEST, preferred_element_type=F32) + carry
        carry = cs[LANES - 1:LANES, :]
        pieces = _split3(cs * LOG2E)
        for h in range(N_HEADS):
            cols = slice(h * HEAD_DIM, (h + 1) * HEAD_DIM)
            qx = jnp.where((lane >= N_SPLIT) & (lane < 2 * N_SPLIT), 1.0, 0.0)
            kx = jnp.where(lane < N_SPLIT, 1.0, 0.0)
            for s, piece in enumerate(pieces):
                col = piece[:, h:h + 1]
                qx = jnp.where(lane == s, col, qx)
                kx = jnp.where(lane == N_SPLIT + s, -col, kx)
            qx_ref[rows, cols] = qx.astype(BF16)
            kx_ref[rows, cols] = kx.astype(BF16)


def decay_features(lf, batch, t):
    m = lf.shape[0]
    out = pl.BlockSpec((t, ATT_WIDTH), lambda b: (b, 0))
    return pl.pallas_call(
        functools.partial(_decay_features_kernel, t=t),
        grid=(batch,),
        in_specs=[pl.BlockSpec((t, LANES), lambda b: (b, 0))],
        out_specs=[out, out],
        out_shape=[jax.ShapeDtypeStruct((m, ATT_WIDTH), BF16)] * 2,
        compiler_params=_params("parallel"),
        name="decay_features",
    )(lf)


def _flash_kernel(qi_ref, kj_ref, q_ref, qx_ref, k_ref, kx_ref, v_ref, o_ref, m_scr, l_scr, acc_scr,
                  *, tq, tk):
    step_id = pl.program_id(1)
    qi = qi_ref[step_id]
    kj = kj_ref[step_id]
    d = HEAD_DIM
    last = (qi * tq) // tk

    @pl.when(kj == 0)
    def _():
        m_scr[...] = jnp.full(m_scr.shape, NEG_INF, F32)
        l_scr[...] = jnp.zeros(l_scr.shape, F32)
        acc_scr[...] = jnp.zeros(acc_scr.shape, F32)

    def step(masked):
        if masked:
            row = lax.broadcasted_iota(jnp.int32, (tq, tk), 0) + (qi * tq - kj * tk)
            col = lax.broadcasted_iota(jnp.int32, (tq, tk), 1)
            keep = col <= row
        for h in range(N_HEADS):
            cs = slice(h * d, (h + 1) * d)
            qa = jnp.concatenate([q_ref[:, cs], qx_ref[:, cs]], axis=1)
            ka = jnp.concatenate([k_ref[:, cs], kx_ref[:, cs]], axis=1)
            s = lax.dot_general(qa, ka, (((1,), (1,)), ((), ())), preferred_element_type=F32)
            if masked:
                s = jnp.where(keep, s, NEG_INF)
            m_old = m_scr[h]
            m_new = jnp.maximum(m_old, jnp.max(s, axis=-1, keepdims=True))
            p = jnp.exp2(s - pltpu.repeat(m_new, tk // LANES, axis=1))
            alpha = jnp.exp2(m_old - m_new)
            l_scr[h] = alpha * l_scr[h] + jnp.sum(p, axis=-1, keepdims=True)
            m_scr[h] = m_new
            acc_scr[:, cs] = alpha * acc_scr[:, cs] + _dot(p.astype(BF16), v_ref[:, cs])

    @pl.when(kj < last)
    def _():
        step(False)

    @pl.when(kj == last)
    def _():
        step(True)
        for h in range(N_HEADS):
            cs = slice(h * d, (h + 1) * d)
            o_ref[:, cs] = acc_scr[:, cs] / l_scr[h]


def prompt_attention(q, qx, k, kx, v, batch, t, tq, tk):
    m = q.shape[0]
    nq, nk = t // tq, t // tk
    pairs = [(i, j) for i in range(nq) for j in range((i * tq) // tk + 1)]
    qi_tab = jnp.asarray([p[0] for p in pairs], jnp.int32)
    kj_tab = jnp.asarray([p[1] for p in pairs], jnp.int32)
    qrow = lambda b, s, qi, kj: (b * nq + qi[s], 0)
    krow = lambda b, s, qi, kj: (b * nk + kj[s], 0)
    qspec = pl.BlockSpec((tq, ATT_WIDTH), qrow)
    kspec = pl.BlockSpec((tk, ATT_WIDTH), krow)
    grid_spec = pltpu.PrefetchScalarGridSpec(
        num_scalar_prefetch=2,
        grid=(batch, len(pairs)),
        in_specs=[qspec, qspec, kspec, kspec, kspec],
        out_specs=qspec,
        scratch_shapes=[pltpu.VMEM((N_HEADS, tq, LANES), F32),
                        pltpu.VMEM((N_HEADS, tq, LANES), F32),
                        pltpu.VMEM((tq, ATT_WIDTH), F32)],
    )
    return pl.pallas_call(
        functools.partial(_flash_kernel, tq=tq, tk=tk),
        grid_spec=grid_spec,
        out_shape=jax.ShapeDtypeStruct((m, ATT_WIDTH), F32),
        compiler_params=_params("parallel", "arbitrary"),
        name="prompt_attention",
    )(qi_tab, kj_tab, q, qx, k, kx, v)


PAGES_PER_STEP = 8


def _decode_kernel(pt_ref, q_ref, kn_ref, vn_ref, lfn_ref, *refs, n_steps):
    del pt_ref
    npg = PAGES_PER_STEP
    k_refs, v_refs, lp_refs = refs[0:npg], refs[npg:2 * npg], refs[2 * npg:3 * npg]
    o_ref, s_scr, piece_scr, m_scr, l_scr, tot_scr, acc_scr = refs[3 * npg:]
    s_id = pl.program_id(1)
    ones = jnp.ones((2 * HEAD_DIM, LANES), BF16)
    qs = q_ref[...].astype(F32)

    @pl.when(s_id == 0)
    def _():
        prod = (kn_ref[...].astype(F32) * qs).astype(BF16)
        m_scr[...] = _dot(prod, ones[0:HEAD_DIM, :])
        l_scr[...] = jnp.ones(l_scr.shape, F32)
        acc_scr[...] = vn_ref[...].astype(F32)
        tot_scr[...] = lfn_ref[...]
        piece_scr[...] = jnp.zeros(piece_scr.shape, F32)

    r = lax.broadcasted_iota(jnp.int32, (PAGE_SIZE, PAGE_SIZE), 0)
    c = lax.broadcasted_iota(jnp.int32, (PAGE_SIZE, PAGE_SIZE), 1)
    later_t = (r > c).astype(F32)
    hh = lax.broadcasted_iota(jnp.int32, (N_HEADS, LANES), 0)
    ll = lax.broadcasted_iota(jnp.int32, (N_HEADS, LANES), 1)
    own = ((ll % N_HEADS) == hh) & (ll < N_SPLIT * N_HEADS)

    tot = tot_scr[...]
    for j in range(npg):
        lp_t = lp_refs[j][...]
        dec_t = jnp.dot(lp_t, later_t, precision=HIGHEST, preferred_element_type=F32) + tot
        tot = tot + jnp.sum(lp_t, axis=1, keepdims=True)
        for s, piece in enumerate(_split3(dec_t * LOG2E)):
            piece_scr[j, s * N_HEADS:(s + 1) * N_HEADS, :] = piece
        pieces = piece_scr[j].T
        spread = jnp.broadcast_to(pieces[:, None, :], (PAGE_SIZE, N_HEADS, LANES))
        extra = jnp.where(own[None], spread, 0.0).astype(BF16).reshape(PAGE_SIZE * N_HEADS, LANES)
        prod = (k_refs[j][...] * qs[None]).astype(BF16).reshape(PAGE_SIZE * N_HEADS, HEAD_DIM)
        s_scr[j] = _dot(jnp.concatenate([prod, extra], axis=1), ones)

    m = m_scr[...]
    l = l_scr[...]
    acc = acc_scr[...]
    for j in range(npg):
        s2 = s_scr[j].reshape(PAGE_SIZE, N_HEADS, LANES)
        m_new = jnp.maximum(m, jnp.max(s2, axis=0))
        alpha = jnp.exp2(m - m_new)
        p = jnp.exp2(s2 - m_new[None])
        l = alpha * l + jnp.sum(p, axis=0)
        acc = alpha * acc + jnp.sum(p * v_refs[j][...], axis=0)
        m = m_new
    tot_scr[...] = tot
    m_scr[...] = m
    l_scr[...] = l
    acc_scr[...] = acc

    @pl.when(s_id == n_steps - 1)
    def _():
        o_ref[...] = acc / l


def sample_attention(q, k_new, v_new, lf_new, cache_k, cache_v, cache_lf_t, page_table):
    db, n_pages = page_table.shape
    npg = PAGES_PER_STEP
    n_steps = n_pages // npg
    heads = pl.BlockSpec((None, N_HEADS, HEAD_DIM), lambda b, s, pt: (b, 0, 0))

    def page(j):
        return lambda b, s, pt: (pt[b, n_pages - 1 - (npg * s + j)], 0, 0, 0)

    def lf_page(j):
        return lambda b, s, pt: (pt[b, n_pages - 1 - (npg * s + j)], 0, 0)

    kv_specs = [pl.BlockSpec((None, PAGE_SIZE, N_HEADS, HEAD_DIM), page(j)) for j in range(npg)]
    lp_specs = [pl.BlockSpec((None, N_HEADS, PAGE_SIZE), lf_page(j)) for j in range(npg)]
    stat = pltpu.VMEM((N_HEADS, LANES), F32)
    grid_spec = pltpu.PrefetchScalarGridSpec(
        num_scalar_prefetch=1,
        grid=(db, n_steps),
        in_specs=[heads, heads, heads, heads] + kv_specs + kv_specs + lp_specs,
        out_specs=heads,
        scratch_shapes=[pltpu.VMEM((npg, PAGE_SIZE * N_HEADS, LANES), F32),
                        pltpu.VMEM((npg, PAGE_SIZE, LANES), F32), stat, stat, stat, stat],
    )
    out = pl.pallas_call(
        functools.partial(_decode_kernel, n_steps=n_steps),
        grid_spec=grid_spec,
        out_shape=jax.ShapeDtypeStruct((db, N_HEADS, HEAD_DIM), F32),
        compiler_params=_params("parallel", "arbitrary"),
        name="sample_attention",
    )(page_table, q.reshape(db, N_HEADS, HEAD_DIM), k_new.reshape(db, N_HEADS, HEAD_DIM),
      v_new.reshape(db, N_HEADS, HEAD_DIM), lf_new,
      *([cache_k] * npg), *([cache_v] * npg), *([cache_lf_t] * npg))
    return out.reshape(db, ATT_WIDTH)


def _mix_store(ao, conv, gb, gao_ref, gco_ref, o_ref):
    conv_o = gb * conv
    o_ref[:, 0:ATT_WIDTH] = ((ao * _row_rms_scale(ao)) * gao_ref[...]).astype(BF16)
    o_ref[:, ATT_WIDTH:D_MODEL] = ((conv_o * _row_rms_scale(conv_o)) * gco_ref[...]).astype(BF16)


def _mix_prompt_kernel(ao_ref, gb_ref, ci_ref, halo_ref, w_ref, gao_ref, gco_ref, o_ref, *,
                       tm, tiles_per_seq):
    ci = ci_ref[...]
    first = pl.program_id(0) % tiles_per_seq == 0
    halo = jnp.where(first, 0.0, halo_ref[...])
    row = lax.broadcasted_iota(jnp.int32, ci.shape, 0)
    prev1 = jnp.where(row == 0, halo[7:8, :], pltpu.roll(ci, 1, axis=0))
    prev2 = jnp.where(row == 0, halo[6:7, :],
                      jnp.where(row == 1, halo[7:8, :], pltpu.roll(ci, 2, axis=0)))
    conv = w_ref[0:1, :] * prev2 + w_ref[1:2, :] * prev1 + w_ref[2:3, :] * ci
    _mix_store(ao_ref[...], conv, gb_ref[...], gao_ref, gco_ref, o_ref)


def mix_prompt(ao, gb, ci, conv_w, g_ao, g_co, t, tm):
    m = ao.shape[0]
    rows = lambda i: (i, 0)
    fixed = lambda i: (0, 0)
    tile = pl.BlockSpec((tm, ATT_WIDTH), rows)
    halo = pl.BlockSpec((8, CONV_WIDTH), lambda i: (jnp.maximum(i * (tm // 8) - 1, 0), 0))
    return pl.pallas_call(
        functools.partial(_mix_prompt_kernel, tm=tm, tiles_per_seq=t // tm),
        grid=(m // tm,),
        in_specs=[tile, tile, tile, halo,
                  pl.BlockSpec((CONV_K, CONV_WIDTH), fixed),
                  pl.BlockSpec((1, ATT_WIDTH), fixed),
                  pl.BlockSpec((1, CONV_WIDTH), fixed)],
        out_specs=pl.BlockSpec((tm, D_MODEL), rows),
        out_shape=jax.ShapeDtypeStruct((m, D_MODEL), BF16),
        compiler_params=_params("parallel"),
        name="mix_prompt",
    )(ao, gb, ci, ci, conv_w, g_ao.reshape(1, -1), g_co.reshape(1, -1))


def _mix_sample_kernel(ao_ref, gb_ref, ci_ref, s0_ref, s1_ref, w_ref, gao_ref, gco_ref, o_ref):
    conv = w_ref[0:1, :] * s0_ref[...] + w_ref[1:2, :] * s1_ref[...] + w_ref[2:3, :] * ci_ref[...]
    _mix_store(ao_ref[...], conv, gb_ref[...], gao_ref, gco_ref, o_ref)


def mix_sample(ao, gb, ci, s0, s1, conv_w, g_ao, g_co):
    m = ao.shape[0]
    full = lambda shape: pl.BlockSpec(shape, lambda i: (0, 0))
    tile = full((m, ATT_WIDTH))
    return pl.pallas_call(
        _mix_sample_kernel,
        grid=(1,),
        in_specs=[tile] * 5 + [full((CONV_K, CONV_WIDTH)), full((1, ATT_WIDTH)), full((1, CONV_WIDTH))],
        out_specs=full((m, D_MODEL)),
        out_shape=jax.ShapeDtypeStruct((m, D_MODEL), BF16),
        compiler_params=_params("arbitrary"),
        name="mix_sample",
    )(ao, gb, ci, s0, s1, conv_w, g_ao.reshape(1, -1), g_co.reshape(1, -1))


def _sample_col(nj):
    return lambda i, j, *_: (0, jnp.where(i == 0, j, nj - 1))


def _first_row_tile():
    return pl.program_id(0) == 0


def _row_resident(shape, index_map):
    return pl.BlockSpec(shape, index_map, pipeline_mode=pl.Buffered(1))


def _mm_resid_kernel(a_ref, as_ref, w_ref, r_ref, rs_ref, o_ref, os_ref):
    o_ref[...] = r_ref[...] + _dot(a_ref[...], w_ref[...].astype(BF16))

    @pl.when(_first_row_tile())
    def _():
        os_ref[...] = rs_ref[...] + _dot(as_ref[...], w_ref[...].astype(BF16))


def matmul_residual(a, a_s, w, resid, resid_s, tm, tn):
    m, k = a.shape
    ms = a_s.shape[0]
    n = w.shape[1]
    scol = _sample_col(n // tn)
    return pl.pallas_call(
        _mm_resid_kernel,
        grid=(m // tm, n // tn),
        in_specs=[pl.BlockSpec((tm, k), lambda i, j: (i, 0)),
                  _row_resident((ms, k), lambda i, j: (0, 0)),
                  pl.BlockSpec((k, tn), lambda i, j: (0, j)),
                  pl.BlockSpec((tm, tn), lambda i, j: (i, j)),
                  pl.BlockSpec((ms, tn), scol)],
        out_specs=[pl.BlockSpec((tm, tn), lambda i, j: (i, j)),
                   pl.BlockSpec((ms, tn), scol)],
        out_shape=[jax.ShapeDtypeStruct((m, n), F32), jax.ShapeDtypeStruct((ms, n), F32)],
        compiler_params=_params("arbitrary", "arbitrary"),
        name="matmul_residual",
    )(a, a_s, w, resid, resid_s)


def _relu2(z):
    z = jnp.maximum(z, 0.0)
    return (z * z).astype(BF16)


def _mm_relu2_kernel(a_ref, as_ref, w_ref, o_ref, os_ref):
    o_ref[...] = _relu2(_dot(a_ref[...], w_ref[...].astype(BF16)))

    @pl.when(_first_row_tile())
    def _():
        os_ref[...] = _relu2(_dot(as_ref[...], w_ref[...].astype(BF16)))


def matmul_relu2(a, a_s, w, tm, tn):
    m, k = a.shape
    ms = a_s.shape[0]
    n = w.shape[1]
    scol = _sample_col(n // tn)
    return pl.pallas_call(
        _mm_relu2_kernel,
        grid=(m // tm, n // tn),
        in_specs=[pl.BlockSpec((tm, k), lambda i, j: (i, 0)),
                  _row_resident((ms, k), lambda i, j: (0, 0)),
                  pl.BlockSpec((k, tn), lambda i, j: (0, j))],
        out_specs=[pl.BlockSpec((tm, tn), lambda i, j: (i, j)),
                   pl.BlockSpec((ms, tn), scol)],
        out_shape=[jax.ShapeDtypeStruct((m, n), BF16), jax.ShapeDtypeStruct((ms, n), BF16)],
        compiler_params=_params("arbitrary", "arbitrary"),
        name="matmul_relu2",
    )(a, a_s, w)


def _mm_down_kernel(a_ref, as_ref, w_ref, r_ref, rs_ref, o_ref, os_ref, acc_ref, accs_ref, *, nk):
    kk = pl.program_id(2)

    @pl.when(kk == 0)
    def _():
        acc_ref[...] = r_ref[...]

    acc_ref[...] += _dot(a_ref[...], w_ref[...].astype(BF16))

    @pl.when(kk == nk - 1)
    def _():
        o_ref[...] = acc_ref[...]

    @pl.when(_first_row_tile())
    def _():
        @pl.when(kk == 0)
        def _():
            accs_ref[...] = rs_ref[...]

        accs_ref[...] += _dot(as_ref[...], w_ref[...].astype(BF16))

        @pl.when(kk == nk - 1)
        def _():
            os_ref[...] = accs_ref[...]


def matmul_down_residual(a, a_s, w, resid, resid_s, tm, tn, tk):
    m, k = a.shape
    ms = a_s.shape[0]
    n = w.shape[1]
    nk = k // tk
    scol = _sample_col(n // tn)
    return pl.pallas_call(
        functools.partial(_mm_down_kernel, nk=nk),
        grid=(m // tm, n // tn, nk),
        in_specs=[pl.BlockSpec((tm, tk), lambda i, j, kk: (i, kk)),
                  pl.BlockSpec((ms, tk), lambda i, j, kk: (0, jnp.where(i == 0, kk, nk - 1))),
                  pl.BlockSpec((tk, tn), lambda i, j, kk: (kk, j)),
                  pl.BlockSpec((tm, tn), lambda i, j, kk: (i, j)),
                  pl.BlockSpec((ms, tn), scol)],
        out_specs=[pl.BlockSpec((tm, tn), lambda i, j, kk: (i, j)),
                   pl.BlockSpec((ms, tn), scol)],
        out_shape=[jax.ShapeDtypeStruct((m, n), F32), jax.ShapeDtypeStruct((ms, n), F32)],
        scratch_shapes=[pltpu.VMEM((tm, tn), F32), pltpu.VMEM((ms, tn), F32)],
        compiler_params=_params("arbitrary", "arbitrary", "arbitrary"),
        name="matmul_down_residual",
    )(a, a_s, w, resid, resid_s)


def _gated(u, wg, p, we, h):
    gate = jax.nn.sigmoid(_dot(u, wg.astype(BF16)))
    return h + gate * _dot(p.astype(BF16), we.astype(BF16))


def _ple_kernel(u_ref, us_ref, wg_ref, p_ref, ps_ref, we_ref, h_ref, hs_ref, o_ref, os_ref):
    o_ref[...] = _gated(u_ref[...], wg_ref[...], p_ref[...], we_ref[...], h_ref[...])

    @pl.when(_first_row_tile())
    def _():
        os_ref[...] = _gated(us_ref[...], wg_ref[...], ps_ref[...], we_ref[...], hs_ref[...])


def gated_embedding(u, u_s, w_pg, p, p_s, w_pe, h, h_s, tm, tn):
    m, k = u.shape
    ms = u_s.shape[0]
    n = w_pg.shape[1]
    kp = p.shape[1]
    scol = _sample_col(n // tn)
    return pl.pallas_call(
        _ple_kernel,
        grid=(m // tm, n // tn),
        in_specs=[pl.BlockSpec((tm, k), lambda i, j: (i, 0)),
                  _row_resident((ms, k), lambda i, j: (0, 0)),
                  pl.BlockSpec((k, tn), lambda i, j: (0, j)),
                  pl.BlockSpec((tm, kp), lambda i, j: (i, 0)),
                  pl.BlockSpec((ms, kp), lambda i, j: (0, 0)),
                  pl.BlockSpec((kp, tn), lambda i, j: (0, j)),
                  pl.BlockSpec((tm, tn), lambda i, j: (i, j)),
                  pl.BlockSpec((ms, tn), scol)],
        out_specs=[pl.BlockSpec((tm, tn), lambda i, j: (i, j)),
                   pl.BlockSpec((ms, tn), scol)],
        out_shape=[jax.ShapeDtypeStruct((m, n), F32), jax.ShapeDtypeStruct((ms, n), F32)],
        compiler_params=_params("arbitrary", "arbitrary"),
        name="gated_embedding",
    )(u, u_s, w_pg, p, p_s, w_pe, h, h_s)


def kernel(x_prompt, x_sample, cache_k, cache_v, cache_logf, state_conv, page_table,
           p_prompt, p_sample, g_mix, w_in, b_f, q_gain, k_gain, conv_w,
           g_attn_out, g_conv_out, w_out, g_mlp, w_up, w_down, g_ple, w_pg, w_pe):
    assert w_in.shape[0] == 1, "single-layer trunk"
    batch, seq, _ = x_prompt.shape
    db = x_sample.shape[0]
    tm = 1024
    tr = 512

    w_in_t = jnp.transpose(w_in[0])
    b_fp = jnp.pad(b_f[0], (0, LANES - N_HEADS)).reshape(1, LANES)

    xp = x_prompt.reshape(batch * seq, D_MODEL)
    xs = x_sample.reshape(db, D_MODEL)
    u, lf = rmsnorm_forget(xp, g_mix[0], w_in_t, b_fp, tr)
    us, lfs = rmsnorm_forget(xs, g_mix[0], w_in_t, b_fp, db)
    (q, kf, kb, vf, vb, gb, ci), (qs, kfs, kbs, vfs, vbs, gbs, cis) = in_projection(
        u, us, w_in_t, q_gain[0], k_gain[0], tm)

    lfs_rep = jnp.broadcast_to(lfs[:, :N_HEADS, None], (db, N_HEADS, LANES))
    cache_lf_t = jnp.transpose(cache_logf[0], (0, 2, 1))
    aos = sample_attention(qs, kbs, vbs, lfs_rep, cache_k[0], cache_v[0], cache_lf_t, page_table)
    st = state_conv[0]
    os_ = mix_sample(aos, gbs, cis, st[:, 0, :], st[:, 1, :], conv_w[0], g_attn_out[0], g_conv_out[0])

    qx, kx = decay_features(lf, batch, seq)
    ao = prompt_attention(q, qx, kb, kx, vb, batch, seq, 256, 256)
    o = mix_prompt(ao, gb, ci, conv_w[0], g_attn_out[0], g_conv_out[0], seq, 256)

    h, hs = matmul_residual(o, os_, w_out[0], xp, xs, tm, 512)
    a, a_s = matmul_relu2(rmsnorm_bf16(h, g_mlp[0], tr), rmsnorm_bf16(hs, g_mlp[0], db),
                          w_up[0], tm, 512)
    h, hs = matmul_down_residual(a, a_s, w_down[0], h, hs, tm, 1024, 2048)
    yp, ys = gated_embedding(rmsnorm_bf16(h, g_ple[0], tr), rmsnorm_bf16(hs, g_ple[0], db),
                             w_pg[0], p_prompt[0].reshape(batch * seq, PLE_DIM),
                             p_sample[0].reshape(db, PLE_DIM), w_pe[0], h, hs, tm, 512)

    k_prompt = kf.reshape(1, batch, seq, N_HEADS, HEAD_DIM)
    v_prompt = vf.reshape(1, batch, seq, N_HEADS, HEAD_DIM)
    logf_prompt = lf[:, :N_HEADS].reshape(1, batch, seq, N_HEADS)
    conv_prompt = ci.reshape(batch, seq, CONV_WIDTH)[None, :, seq - (CONV_K - 1):, :]
    k_sample = kfs.reshape(1, db, 1, N_HEADS, HEAD_DIM)
    v_sample = vfs.reshape(1, db, 1, N_HEADS, HEAD_DIM)
    logf_sample = lfs[:, :N_HEADS].reshape(1, db, 1, N_HEADS)
    conv_sample = jnp.stack([st[:, 1, :], cis], axis=1)[None]

    return (yp.reshape(batch, seq, D_MODEL), ys.reshape(db, 1, D_MODEL),
            k_prompt, v_prompt, logf_prompt, conv_prompt,
            k_sample, v_sample, logf_sample, conv_sample)
```
